```python
import math
import jax, jax.numpy as jnp
from jax import lax
import numpy as np

D_MODEL = 1024
BATCH = 8
SEQ = 2048
DEPTH = 2
DEC_BATCH = 4
DEC_SEQ = 4096
PAST_LEN = 128

GRID_W = 64
MLA_HEADS = 8
MLA_NOPE = 64
MLA_ROPE = 32
MLA_V = 64
Q_LORA = 384
KV_LORA = 256
ROPE_BASE = 10000.0
Q_BLOCK = 128
NA_HEADS = 8
NA_HD = 64
NA_KH = 8
NA_KW = 16
D_RNN = 1024
LRU_BLOCKS = 8
LRU_BW = D_RNN // LRU_BLOCKS
REC_CONV = 4
LRU_C = 8.0
D_FF = 4096
FFN_CONV = 3
EPS = 1e-6

ATTN_IN = Q_LORA + KV_LORA + MLA_ROPE + 3 * NA_HEADS * NA_HD
ATTN_OUT = MLA_HEADS * MLA_V + NA_HEADS * NA_HD
N_EVEN = (DEPTH + 1) // 2
N_ODD = DEPTH // 2

kernel_name = 'hybrid_mla_natten_rglru_encoder'


def rmsnorm(x, g):
    xf = x.astype(jnp.float32)
    y = xf * lax.rsqrt(jnp.mean(xf * xf, axis=-1, keepdims=True) + EPS)
    return (y * g.astype(jnp.float32)).astype(x.dtype)


def dwconv(x, w, b):
    K = w.shape[0]
    S = x.shape[1]
    left = (K - 1) // 2
    xp = jnp.pad(x, ((0, 0), (left, K - 1 - left), (0, 0)))
    y = w[0] * xp[:, 0:S]
    for k in range(1, K):
        y = y + w[k] * xp[:, k:k + S]
    return y + b


def rope(x, cos, sin):
    half = x.shape[-1] // 2
    x1, x2 = x[..., :half], x[..., half:]
    cos = cos.astype(x.dtype)
    sin = sin.astype(x.dtype)
    return jnp.concatenate([x1 * cos - x2 * sin, x2 * cos + x1 * sin], axis=-1)


def mla(q_lat, kv_lat, k_pe_raw, q_norm, w_uq, kv_norm, w_ukv):
    B, S, _ = q_lat.shape
    q = (rmsnorm(q_lat, q_norm) @ w_uq).reshape(B, S, MLA_HEADS, MLA_NOPE + MLA_ROPE)
    kv = (rmsnorm(kv_lat, kv_norm) @ w_ukv).reshape(B, S, MLA_HEADS, MLA_NOPE + MLA_V)
    q_nope, q_pe = q[..., :MLA_NOPE], q[..., MLA_NOPE:]
    k_nope, v = kv[..., :MLA_NOPE], kv[..., MLA_NOPE:]
    pos = jnp.arange(S, dtype=jnp.float32)
    inv = ROPE_BASE ** (-jnp.arange(0, MLA_ROPE, 2, dtype=jnp.float32) / MLA_ROPE)
    ang = pos[:, None] * inv[None, :]
    cos, sin = jnp.cos(ang), jnp.sin(ang)
    q_pe = rope(q_pe, cos[:, None, :], sin[:, None, :])
    k_pe = rope(k_pe_raw, cos, sin)
    scale = (MLA_NOPE + MLA_ROPE) ** -0.5
    nqb = S // Q_BLOCK

    def q_block(xq):
        qn, qp = xq
        s = (jnp.einsum('bqhd,bkhd->bhqk', qn, k_nope)
             + jnp.einsum('bqhr,bkr->bhqk', qp, k_pe))
        p = jax.nn.softmax(s.astype(jnp.float32) * scale, axis=-1).astype(v.dtype)
        return jnp.einsum('bhqk,bkhv->bqhv', p, v)

    def to_blocks(t):
        return jnp.moveaxis(t.reshape(B, nqb, Q_BLOCK, *t.shape[2:]), 1, 0)

    o = lax.map(q_block, (to_blocks(q_nope), to_blocks(q_pe)))
    return jnp.moveaxis(o, 0, 1).reshape(B, S, MLA_HEADS * MLA_V)


def neighborhood_attn(q, k, v, rpb):
    B, S, H, d = q.shape
    rows = S // GRID_W
    kh = min(NA_KH, rows)
    qg = q.reshape(B, rows, GRID_W, H, d)
    kg = k.reshape(B, rows, GRID_W, H, d)
    vg = v.reshape(B, rows, GRID_W, H, d)
    cols = np.arange(GRID_W)
    col_start = np.clip(cols - NA_KW // 2, 0, GRID_W - NA_KW)
    col_idx = col_start[:, None] + np.arange(NA_KW)[None, :]
    col_bias_idx = col_idx - cols[:, None] + (NA_KW - 1)
    scale = d ** -0.5

    def row_block(r):
        rs = jnp.clip(r - kh // 2, 0, rows - kh)
        q_row = lax.dynamic_index_in_dim(qg, r, axis=1, keepdims=False)
        k_win = lax.dynamic_slice_in_dim(kg, rs, kh, axis=1)[:, :, col_idx]
        v_win = lax.dynamic_slice_in_dim(vg, rs, kh, axis=1)[:, :, col_idx]
        row_bias_idx = rs + jnp.arange(kh) - r + (NA_KH - 1)
        bias = rpb[:, row_bias_idx[None, :, None], col_bias_idx[:, None, :]]
        s = (jnp.einsum('bqhd,bkqwhd->bhqkw', q_row, k_win).astype(jnp.float32) * scale
             + bias.astype(jnp.float32)[None])
        p = jax.nn.softmax(s.reshape(B, H, GRID_W, kh * NA_KW), axis=-1)
        p = p.reshape(s.shape).astype(v.dtype)
        return jnp.einsum('bhqkw,bkqwhd->bqhd', p, v_win)

    o = lax.map(row_block, jnp.arange(rows))
    return jnp.moveaxis(o, 0, 1).reshape(B, S, H * d)


def attn_mixer(h, w_in, q_norm, w_uq, kv_norm, w_ukv, rpb, w_out):
    B, S, _ = h.shape
    z = h @ w_in
    q_lat, kv_lat, k_pe, na_qkv = jnp.split(
        z, [Q_LORA, Q_LORA + KV_LORA, Q_LORA + KV_LORA + MLA_ROPE], axis=-1)
    na_qkv = na_qkv.reshape(B, S, 3, NA_HEADS, NA_HD)
    o_mla = mla(q_lat, kv_lat, k_pe, q_norm, w_uq, kv_norm, w_ukv)
    o_na = neighborhood_attn(na_qkv[:, :, 0], na_qkv[:, :, 1], na_qkv[:, :, 2], rpb)
    return jnp.concatenate([o_mla, o_na], axis=-1) @ w_out


def _lin_combine(left, right):
    a_l, b_l = left
    a_r, b_r = right
    return a_r * a_l, a_r * b_l + b_r


def rglru_mixer(h, w_in, conv_w, conv_b, ga_w, ga_b, gx_w, gx_b, lam, w_out):
    B, S, _ = h.shape
    gate, xb = jnp.split(h @ w_in, 2, axis=-1)
    xb = dwconv(xb, conv_w, conv_b)
    xr = xb.reshape(B, S, LRU_BLOCKS, LRU_BW)
    r = jax.nn.sigmoid((jnp.einsum('bsnc,encd->ebsnd', xr, ga_w).reshape(2, B, S, D_RNN)
                        + ga_b[:, None, None, :]).astype(jnp.float32))
    i = jax.nn.sigmoid((jnp.einsum('bsnc,encd->ebsnd', xr, gx_w).reshape(2, B, S, D_RNN)
                        + gx_b[:, None, None, :]).astype(jnp.float32))
    log_a = -LRU_C * r * jax.nn.softplus(-lam.astype(jnp.float32))[:, None, None, :]
    a = jnp.exp(log_a)
    b = jnp.sqrt(-jnp.expm1(2.0 * log_a)) * (i * xb.astype(jnp.float32)[None])
    h_fwd = lax.associative_scan(_lin_combine, (a[0], b[0]), axis=1)[1]
    h_bwd = lax.associative_scan(_lin_combine, (a[1], b[1]), axis=1, reverse=True)[1]
    y = (h_fwd + h_bwd).astype(h.dtype) * jax.nn.gelu(gate, approximate=True)
    return y @ w_out


def conv_ffn(h, w_up, conv_w, conv_b, w_down):
    u = dwconv(h @ w_up, conv_w, conv_b)
    g, val = jnp.split(u, 2, axis=-1)
    return (jax.nn.gelu(g, approximate=True) * val) @ w_down


def _trunk(x, norm_mix_pre, norm_mix_post, norm_ffn_pre, norm_ffn_post,
           w_in_attn, q_norm, w_uq, kv_norm, w_ukv, na_rpb, w_out_attn,
           w_in_rec, conv_w_rec, conv_b_rec, gate_a_w, gate_a_b, gate_x_w, gate_x_b,
           lru_lambda, w_out_rec, w_ffn_up, conv_w_ffn, conv_b_ffn, w_ffn_down):
    for li in range(DEPTH):
        j = li // 2
        h = rmsnorm(x, norm_mix_pre[li])
        if li % 2 == 0:
            m = attn_mixer(h, w_in_attn[j], q_norm[j], w_uq[j], kv_norm[j], w_ukv[j],
                           na_rpb[j], w_out_attn[j])
        else:
            m = rglru_mixer(h, w_in_rec[j], conv_w_rec[j], conv_b_rec[j], gate_a_w[j],
                            gate_a_b[j], gate_x_w[j], gate_x_b[j], lru_lambda[j], w_out_rec[j])
        x = x + rmsnorm(m, norm_mix_post[li])
        h = rmsnorm(x, norm_ffn_pre[li])
        f = conv_ffn(h, w_ffn_up[li], conv_w_ffn[li], conv_b_ffn[li], w_ffn_down[li])
        x = x + rmsnorm(f, norm_ffn_post[li])
    return x


def setup_inputs(seed: int = 0) -> dict:
    key = jax.random.key(seed)
    ks = jax.random.split(key, 26)
    f32 = jnp.float32

    def nrm(k, shape, s):
        return jax.random.normal(k, shape, f32) * s

    def gain(k, shape):
        return 1.0 + 0.05 * jax.random.normal(k, shape, f32)

    a0 = jax.random.uniform(ks[20], (N_ODD, 2, D_RNN), f32, minval=0.9, maxval=0.999)
    p = a0 ** (1.0 / LRU_C)
    lru_lambda = jnp.log(p) - jnp.log1p(-p)
    return {
        'x_prompt': nrm(ks[0], (BATCH, SEQ, D_MODEL), 1.0),
        'x_sample': nrm(ks[1], (DEC_BATCH, DEC_SEQ, D_MODEL), 1.0),
        'norm_mix_pre': gain(ks[2], (DEPTH, D_MODEL)),
        'norm_mix_post': gain(ks[3], (DEPTH, D_MODEL)),
        'norm_ffn_pre': gain(ks[4], (DEPTH, D_MODEL)),
        'norm_ffn_post': gain(ks[5], (DEPTH, D_MODEL)),
        'w_in_attn': nrm(ks[6], (N_EVEN, D_MODEL, ATTN_IN), D_MODEL ** -0.5),
        'q_norm': gain(ks[7], (N_EVEN, Q_LORA)),
        'w_uq': nrm(ks[8], (N_EVEN, Q_LORA, MLA_HEADS * (MLA_NOPE + MLA_ROPE)), Q_LORA ** -0.5),
        'kv_norm': gain(ks[9], (N_EVEN, KV_LORA)),
        'w_ukv': nrm(ks[10], (N_EVEN, KV_LORA, MLA_HEADS * (MLA_NOPE + MLA_V)), KV_LORA ** -0.5),
        'na_rpb': nrm(ks[11], (N_EVEN, NA_HEADS, 2 * NA_KH - 1, 2 * NA_KW - 1), 0.1),
        'w_out_attn': nrm(ks[12], (N_EVEN, ATTN_OUT, D_MODEL), ATTN_OUT ** -0.5),
        'w_in_rec': nrm(ks[13], (N_ODD, D_MODEL, 2 * D_RNN), D_MODEL ** -0.5),
        'conv_w_rec': nrm(ks[14], (N_ODD, REC_CONV, D_RNN), REC_CONV ** -0.5),
        'conv_b_rec': nrm(ks[15], (N_ODD, D_RNN), 0.01),
        'gate_a_w': nrm(ks[16], (N_ODD, 2, LRU_BLOCKS, LRU_BW, LRU_BW), LRU_BW ** -0.5),
        'gate_a_b': nrm(ks[17], (N_ODD, 2, D_RNN), 0.01),
        'gate_x_w': nrm(ks[18], (N_ODD, 2, LRU_BLOCKS, LRU_BW, LRU_BW), LRU_BW ** -0.5),
        'gate_x_b': nrm(ks[19], (N_ODD, 2, D_RNN), 0.01),
        'lru_lambda': lru_lambda,
        'w_out_rec': nrm(ks[21], (N_ODD, D_RNN, D_MODEL), D_RNN ** -0.5),
        'w_ffn_up': nrm(ks[22], (DEPTH, D_MODEL, 2 * D_FF), D_MODEL ** -0.5),
        'conv_w_ffn': nrm(ks[23], (DEPTH, FFN_CONV, 2 * D_FF), FFN_CONV ** -0.5),
        'conv_b_ffn': nrm(ks[24], (DEPTH, 2 * D_FF), 0.01),
        'w_ffn_down': nrm(ks[25], (DEPTH, D_FF, D_MODEL), D_FF ** -0.5),
    }


def reference(x_prompt, x_sample, norm_mix_pre, norm_mix_post, norm_ffn_pre, norm_ffn_post,
              w_in_attn, q_norm, w_uq, kv_norm, w_ukv, na_rpb, w_out_attn,
              w_in_rec, conv_w_rec, conv_b_rec, gate_a_w, gate_a_b, gate_x_w, gate_x_b,
              lru_lambda, w_out_rec, w_ffn_up, conv_w_ffn, conv_b_ffn, w_ffn_down):
    weights = (norm_mix_pre, norm_mix_post, norm_ffn_pre, norm_ffn_post,
               w_in_attn, q_norm, w_uq, kv_norm, w_ukv, na_rpb, w_out_attn,
               w_in_rec, conv_w_rec, conv_b_rec, gate_a_w, gate_a_b, gate_x_w, gate_x_b,
               lru_lambda, w_out_rec, w_ffn_up, conv_w_ffn, conv_b_ffn, w_ffn_down)
    y_prompt = _trunk(x_prompt, *weights)
    y_sample = _trunk(x_sample, *weights)
    return (y_prompt, y_sample)
```

```python
import functools
import math

import numpy as np
import jax
import jax.numpy as jnp
from jax import lax
from jax.experimental import pallas as pl
from jax.experimental.pallas import tpu as pltpu

F32 = jnp.float32
BF16 = jnp.bfloat16

EPS = 1e-6
GRID_W = 64
MLA_HEADS = 8
MLA_NOPE = 64
MLA_ROPE = 32
MLA_V = 64
ROPE_BASE = 10000.0
NA_HEADS = 8
NA_HD = 64
NA_KH = 8
NA_KW = 16
LRU_C = 8.0

LANES = 128
HALO = 16
NEG = -1e30
VMEM_LIMIT = 56 * 1024 * 1024


def _rms(x, g):
    return x * lax.rsqrt(jnp.mean(x * x, axis=-1, keepdims=True) + EPS) * g


def _gelu_tanh(x):
    return 0.5 * x * (1.0 + jnp.tanh(math.sqrt(2.0 / math.pi) * (x + 0.044715 * (x * x * x))))


def _dot(a, b):
    return jnp.dot(a, b, preferred_element_type=F32)


def _dot_nt(a, b):
    return lax.dot_general(a, b, (((1,), (1,)), ((), ())), preferred_element_type=F32)


def _seq_pos(tok0, n_prompt, s_prompt, s_sample):
    in_p = tok0 < n_prompt
    seq = jnp.where(in_p, s_prompt, s_sample)
    rel = jnp.where(in_p, tok0, tok0 - n_prompt)
    return lax.rem(rel, seq), seq


def _params(sem):
    return pltpu.CompilerParams(dimension_semantics=sem, vmem_limit_bytes=VMEM_LIMIT)


def _const_spec(shape):
    nd = len(shape)
    return pl.BlockSpec(shape, lambda *_: (0,) * nd)


def _attn_in_kernel(x_ref, g_ref, win_ref, qn_ref, wq_ref, wqs_ref, kvn_ref, wk_ref, wv_ref,
                    cos_ref, sin_ref,
                    q_ref, k_ref, v_ref, naq_ref, nak_ref, nav_ref, *, q_lora, kv_lora, q_scale, na_scale):
    h = _rms(x_ref[...], g_ref[...]).astype(BF16)
    z = _dot(h, win_ref[...])
    o0 = q_lora
    o1 = o0 + kv_lora
    o2 = o1 + LANES
    o3 = o2 + LANES
    na_w = NA_HEADS * NA_HD
    cos = cos_ref[...]
    sin = sin_ref[...]

    nq = _rms(z[:, :o0], qn_ref[...]).astype(BF16)
    q = _dot(nq, wq_ref[...])
    qs = _dot(nq, wqs_ref[...])
    for hd in range(MLA_HEADS):
        sl = slice(hd * LANES, (hd + 1) * LANES)
        q_ref[:, sl] = ((q[:, sl] * cos + qs[:, sl] * sin) * q_scale).astype(BF16)

    nkv = _rms(z[:, o0:o1], kvn_ref[...]).astype(BF16)
    k = _dot(nkv, wk_ref[...])
    rot = z[:, o1:o2] * cos + z[:, o2:o3] * sin
    for hd in range(MLA_HEADS):
        sl = slice(hd * LANES, (hd + 1) * LANES)
        k_ref[:, sl] = (k[:, sl] + rot).astype(BF16)
    v_ref[...] = _dot(nkv, wv_ref[...]).astype(BF16)

    naq_ref[...] = (z[:, o3:o3 + na_w] * na_scale).astype(BF16)
    nak_ref[...] = z[:, o3 + na_w:o3 + 2 * na_w].astype(BF16)
    nav_ref[...] = z[:, o3 + 2 * na_w:o3 + 3 * na_w].astype(BF16)


def _attn_in(x, g, w_all, qn, wq, wqs, kvn, wk, wv, cos_t, sin_t, *, tile, geom):
    n, d = x.shape
    n_p, s_p, s_s = geom
    q_lora = qn.shape[1]
    kv_lora = kvn.shape[1]
    hw = MLA_HEADS * LANES
    na_w = NA_HEADS * NA_HD

    def rope_idx(i):
        pos0, _ = _seq_pos(i * tile, n_p, s_p, s_s)
        return (pos0 // tile, 0)

    tok = lambda w: pl.BlockSpec((tile, w), lambda i: (i, 0))
    kern = functools.partial(_attn_in_kernel, q_lora=q_lora, kv_lora=kv_lora,
                             q_scale=float((MLA_NOPE + MLA_ROPE) ** -0.5), na_scale=float(NA_HD ** -0.5))
    return pl.pallas_call(
        kern,
        grid=(n // tile,),
        in_specs=[tok(d), _const_spec(g.shape), _const_spec(w_all.shape), _const_spec(qn.shape),
                  _const_spec(wq.shape), _const_spec(wqs.shape), _const_spec(kvn.shape),
                  _const_spec(wk.shape), _const_spec(wv.shape),
                  pl.BlockSpec((tile, LANES), rope_idx), pl.BlockSpec((tile, LANES), rope_idx)],
        out_specs=[tok(hw), tok(hw), tok(MLA_HEADS * MLA_V), tok(na_w), tok(na_w), tok(na_w)],
        out_shape=[jax.ShapeDtypeStruct((n, hw), BF16), jax.ShapeDtypeStruct((n, hw), BF16),
                   jax.ShapeDtypeStruct((n, MLA_HEADS * MLA_V), BF16),
                   jax.ShapeDtypeStruct((n, na_w), BF16), jax.ShapeDtypeStruct((n, na_w), BF16),
                   jax.ShapeDtypeStruct((n, na_w), BF16)],
        compiler_params=_params(("parallel",)),
        name="attn_in",
    )(x, g, w_all, qn, wq, wqs, kvn, wk, wv, cos_t, sin_t)


def _mla_kernel(*refs, kc, aliased):
    if aliased:
        q_ref, k_ref, v_ref, _, o_ref = refs
    else:
        q_ref, k_ref, v_ref, o_ref = refs
    tq = q_ref.shape[0]
    n_kc = k_ref.shape[0] // kc
    outs = []
    for hd in range(2):
        hsl = slice(hd * LANES, (hd + 1) * LANES)
        qh = q_ref[:, hsl]

        def body(c, carry, hsl=hsl, qh=qh):
            m, l, acc = carry
            ks = pl.multiple_of(c * kc, kc)
            s = _dot_nt(qh, k_ref[pl.ds(ks, kc), hsl])
            m_new = jnp.maximum(m, jnp.max(s, axis=-1, keepdims=True))
            alpha = jnp.exp(m - m_new)
            p = jnp.exp(s - m_new)
            l = alpha * l + jnp.sum(p, axis=-1, keepdims=True)
            acc = alpha * acc + _dot(p.astype(BF16), v_ref[pl.ds(ks, kc), :])
            return m_new, l, acc

        init = (jnp.full((tq, 1), NEG, F32), jnp.zeros((tq, 1), F32), jnp.zeros((tq, LANES), F32))
        _, l, acc = lax.fori_loop(0, n_kc, body, init)
        outs.append(acc / l)
    lane = lax.broadcasted_iota(jnp.int32, (tq, LANES), 1)
    o_ref[...] = jnp.where(lane < MLA_V, outs[0], outs[1]).astype(BF16)


def _mla_group(q, k, v, o_prev, *, tok_base, batch, seq, tq, kc):
    n = q.shape[0]
    qb = tok_base // tq
    sb = tok_base // seq
    nq = seq // tq
    in_specs = [pl.BlockSpec((tq, 2 * LANES), lambda b, hp, i: (qb + b * nq + i, hp)),
                pl.BlockSpec((seq, 2 * LANES), lambda b, hp, i: (sb + b, hp)),
                pl.BlockSpec((seq, LANES), lambda b, hp, i: (sb + b, hp))]
    args = [q, k, v]
    aliases = {}
    if o_prev is not None:
        in_specs.append(pl.BlockSpec(memory_space=pl.ANY))
        args.append(o_prev)
        aliases = {3: 0}
    return pl.pallas_call(
        functools.partial(_mla_kernel, kc=kc, aliased=o_prev is not None),
        grid=(batch, MLA_HEADS // 2, nq),
        in_specs=in_specs,
        out_specs=pl.BlockSpec((tq, LANES), lambda b, hp, i: (qb + b * nq + i, hp)),
        out_shape=jax.ShapeDtypeStruct((n, MLA_HEADS * MLA_V), BF16),
        input_output_aliases=aliases,
        compiler_params=_params(("parallel", "parallel", "arbitrary")),
        name="mla_attn",
    )(*args)


def _na_kernel(*refs, rows, aliased):
    if aliased:
        q_ref, k_ref, v_ref, bias_ref, _, o_ref = refs
    else:
        q_ref, k_ref, v_ref, bias_ref, o_ref = refs
    win = NA_KH * GRID_W
    lane_q = lax.broadcasted_iota(jnp.int32, (GRID_W, LANES), 1)

    def body(r, _):
        rs = jnp.clip(r - NA_KH // 2, 0, rows - NA_KH)
        delta = r - rs
        q0 = pl.multiple_of(r * GRID_W, GRID_W)
        k0 = pl.multiple_of(rs * GRID_W, GRID_W)
        qr = q_ref[pl.ds(q0, GRID_W), :]
        kw = k_ref[pl.ds(k0, win), :]
        vw = v_ref[pl.ds(k0, win), :]
        outs = []
        for hd in range(2):
            in_head = (lane_q < NA_HD) if hd == 0 else (lane_q >= NA_HD)
            qm = jnp.where(in_head, qr, jnp.zeros_like(qr))
            s = _dot_nt(qm, kw) + bias_ref[hd, delta]
            m = jnp.max(s, axis=-1, keepdims=True)
            p = jnp.exp(s - m)
            l = jnp.sum(p, axis=-1, keepdims=True)
            outs.append(_dot(p.astype(BF16), vw) / l)
        o_ref[pl.ds(q0, GRID_W), :] = jnp.where(lane_q < NA_HD, outs[0], outs[1]).astype(BF16)
        return 0

    lax.fori_loop(0, rows, body, 0)


def _na_group(q, k, v, bias, o_prev, *, tok_base, batch, seq):
    n = q.shape[0]
    sb = tok_base // seq
    rows = seq // GRID_W
    blk = pl.BlockSpec((seq, LANES), lambda b, hp: (sb + b, hp))
    in_specs = [blk, blk, blk,
                pl.BlockSpec((2,) + bias.shape[1:], lambda b, hp: (hp, 0, 0, 0))]
    args = [q, k, v, bias]
    aliases = {}
    if o_prev is not None:
        in_specs.append(pl.BlockSpec(memory_space=pl.ANY))
        args.append(o_prev)
        aliases = {4: 0}
    return pl.pallas_call(
        functools.partial(_na_kernel, rows=rows, aliased=o_prev is not None),
        grid=(batch, NA_HEADS // 2),
        in_specs=in_specs,
        out_specs=blk,
        out_shape=jax.ShapeDtypeStruct((n, NA_HEADS * NA_HD), BF16),
        input_output_aliases=aliases,
        compiler_params=_params(("parallel", "parallel")),
        name="na_attn",
    )(*args)


def _na_bias_table(rpb):
    cols = np.arange(GRID_W)
    col_start = np.clip(cols - NA_KW // 2, 0, GRID_W - NA_KW)
    kc = np.arange(GRID_W)
    in_win = (kc[None, :] >= col_start[:, None]) & (kc[None, :] < col_start[:, None] + NA_KW)
    col_rel = np.clip(kc[None, :] - cols[:, None] + (NA_KW - 1), 0, 2 * NA_KW - 2)
    delta = np.arange(NA_KH)
    krow = np.arange(NA_KH)
    row_rel = krow[None, :] - delta[:, None] + (NA_KH - 1)
    t = rpb[:, row_rel[:, None, :, None], col_rel[None, :, None, :]]
    t = jnp.where(jnp.asarray(in_win)[None, None, :, None, :], t.astype(F32), NEG)
    return t.reshape(rpb.shape[0], NA_KH, GRID_W, NA_KH * GRID_W)


def _attn_out_kernel(om_ref, on_ref, wm_ref, wn_ref, x_ref, gpost_ref, gnext_ref, xo_ref, ho_ref):
    m = _dot(om_ref[...], wm_ref[...]) + _dot(on_ref[...], wn_ref[...])
    xo = x_ref[...] + _rms(m, gpost_ref[...])
    xo_ref[...] = xo
    ho_ref[...] = _rms(xo, gnext_ref[...]).astype(BF16)


def _attn_out(o_mla, o_na, w_m, w_n, x, g_post, g_next, *, tile):
    n, d = x.shape
    tok = lambda w: pl.BlockSpec((tile, w), lambda i: (i, 0))
    return pl.pallas_call(
        _attn_out_kernel,
        grid=(n // tile,),
        in_specs=[tok(o_mla.shape[1]), tok(o_na.shape[1]), _const_spec(w_m.shape), _const_spec(w_n.shape),
                  tok(d), _const_spec(g_post.shape), _const_spec(g_next.shape)],
        out_specs=[tok(d), tok(d)],
        out_shape=[jax.ShapeDtypeStruct((n, d), F32), jax.ShapeDtypeStruct((n, d), BF16)],
        compiler_params=_params(("parallel",)),
        name="attn_out",
    )(o_mla, o_na, w_m, w_n, x, g_post, g_next)


def _halo_specs(tile, d, n):
    per = tile // HALO
    last = n // HALO - 1
    return [pl.BlockSpec((HALO, d), lambda i, *_: (jnp.maximum(i * per - 1, 0), 0)),
            pl.BlockSpec((tile, d), lambda i, *_: (i, 0)),
            pl.BlockSpec((HALO, d), lambda i, *_: (jnp.minimum((i + 1) * per, last), 0))]


def _fill_halo(hh_ref, hp_ref, hm_ref, hn_ref, tok0, tile, geom):
    pos0, seq = _seq_pos(tok0, *geom)
    zero = jnp.zeros(hp_ref.shape, hp_ref.dtype)
    hh_ref[0:HALO, :] = jnp.where(pos0 == 0, zero, hp_ref[...])
    hh_ref[HALO:HALO + tile, :] = hm_ref[...]
    hh_ref[HALO + tile:, :] = jnp.where(pos0 + tile == seq, zero, hn_ref[...])


def _dwconv(u, w, b, tile, left):
    y = b
    for kk in range(w.shape[0]):
        s = HALO - left + kk
        y = y + w[kk:kk + 1, :] * u[s:s + tile, :]
    return y


def _ffn_kernel(hp_ref, hm_ref, hn_ref, x_ref, wg_ref, wv_ref, cwg_ref, cwv_ref, cbg_ref, cbv_ref,
                wd_ref, gpost_ref, gnext_ref, xo_ref, ho_ref, hh_ref, acc_ref, *, tile, geom):
    i = pl.program_id(0)
    c = pl.program_id(1)

    @pl.when(c == 0)
    def _():
        _fill_halo(hh_ref, hp_ref, hm_ref, hn_ref, i * tile, tile, geom)
        acc_ref[...] = jnp.zeros(acc_ref.shape, F32)

    hh = hh_ref[...]
    left = (cwg_ref.shape[0] - 1) // 2
    g = _dwconv(_dot(hh, wg_ref[...]), cwg_ref[...], cbg_ref[...], tile, left)
    val = _dwconv(_dot(hh, wv_ref[...]), cwv_ref[...], cbv_ref[...], tile, left)
    a = (_gelu_tanh(g) * val).astype(BF16)
    acc_ref[...] += _dot(a, wd_ref[...])

    @pl.when(c == pl.num_programs(1) - 1)
    def _():
        xo = x_ref[...] + _rms(acc_ref[...], gpost_ref[...])
        xo_ref[...] = xo
        ho_ref[...] = _rms(xo, gnext_ref[...]).astype(BF16)


def _ffn(h, x, w_up, conv_w, conv_b, w_down, g_post, g_next, *, tile, fc, geom):
    n, d = x.shape
    d_ff = w_down.shape[0]
    nc = d_ff // fc
    kk = conv_w.shape[0]
    tok = pl.BlockSpec((tile, d), lambda i, c: (i, 0))
    in_specs = _halo_specs(tile, d, n) + [
        tok,
        pl.BlockSpec((d, fc), lambda i, c: (0, c)),
        pl.BlockSpec((d, fc), lambda i, c: (0, nc + c)),
        pl.BlockSpec((kk, fc), lambda i, c: (0, c)),
        pl.BlockSpec((kk, fc), lambda i, c: (0, nc + c)),
        pl.BlockSpec((1, fc), lambda i, c: (0, c)),
        pl.BlockSpec((1, fc), lambda i, c: (0, nc + c)),
        pl.BlockSpec((fc, d), lambda i, c: (c, 0)),
        _const_spec(g_post.shape), _const_spec(g_next.shape)]
    return pl.pallas_call(
        functools.partial(_ffn_kernel, tile=tile, geom=geom),
        grid=(n // tile, nc),
        in_specs=in_specs,
        out_specs=[tok, tok],
        out_shape=[jax.ShapeDtypeStruct((n, d), F32), jax.ShapeDtypeStruct((n, d), BF16)],
        scratch_shapes=[pltpu.VMEM((tile + 2 * HALO, d), BF16), pltpu.VMEM((tile, d), F32)],
        compiler_params=_params(("parallel", "arbitrary")),
        name="conv_ffn",
    )(h, h, h, x, w_up, w_up, conv_w, conv_w, conv_b, conv_b, w_down, g_post, g_next)


SUB = 8


def _lru_coeffs(xc, xc_bf, gw_ref, gab_ref, gxb_ref, lam_ref):
    nb, bw, _ = gw_ref.shape
    lam = lam_ref[...]
    neg = -lam
    sp = jnp.maximum(neg, 0.0) + jnp.log1p(jnp.exp(-jnp.abs(neg)))
    a_parts, b_parts = [], []
    for blk in range(nb):
        sl = slice(blk * bw, (blk + 1) * bw)
        rg = _dot(xc_bf[:, sl], gw_ref[blk])
        r = jax.nn.sigmoid(rg[:, :bw] + gab_ref[:, sl])
        gi = jax.nn.sigmoid(rg[:, bw:] + gxb_ref[:, sl])
        log_a = (-LRU_C) * r * sp[:, sl]
        t = jnp.tanh(log_a)
        one_minus_a2 = (-2.0 * t) / (1.0 - t)
        a_parts.append(jnp.exp(log_a))
        b_parts.append(jnp.sqrt(one_minus_a2) * (gi * xc[:, sl]))
    return jnp.concatenate(a_parts, axis=1), jnp.concatenate(b_parts, axis=1)


def _scan_tile(a_ref, b_ref, carry_ref, *, reverse):
    t, ch = a_ref.shape
    n_chunks = t // SUB
    row = lax.broadcasted_iota(jnp.int32, (SUB, ch), 0)

    def body(j, carry):
        jj = (n_chunks - 1 - j) if reverse else j
        r0 = pl.multiple_of(jj * SUB, SUB)
        a = a_ref[pl.ds(r0, SUB), :]
        b = b_ref[pl.ds(r0, SUB), :]
        for dist in (1, 2, 4):
            shift = (SUB - dist) if reverse else dist
            a_sh = pltpu.roll(a, shift, axis=0)
            b_sh = pltpu.roll(b, shift, axis=0)
            valid = (row < SUB - dist) if reverse else (row >= dist)
            b = jnp.where(valid, a * b_sh + b, b)
            a = jnp.where(valid, a * a_sh, a)
        hcur = a * carry + b
        b_ref[pl.ds(r0, SUB), :] = hcur
        edge = hcur[0:1, :] if reverse else hcur[SUB - 1:SUB, :]
        return jnp.broadcast_to(edge, (SUB, ch))

    carry_ref[...] = lax.fori_loop(0, n_chunks, body, carry_ref[...])


def _rec_fwd_kernel(hp_ref, hm_ref, hn_ref, wg_ref, wx_ref, cw_ref, cb_ref, gw_ref, gab_ref, gxb_ref,
                    lam_ref, gate_ref, xc_ref, hf_ref, hh_ref, a_ref, b_ref, carry_ref, *, tile, geom):
    i = pl.program_id(0)
    pos0, _ = _seq_pos(i * tile, *geom)
    _fill_halo(hh_ref, hp_ref, hm_ref, hn_ref, i * tile, tile, geom)

    @pl.when(pos0 == 0)
    def _():
        carry_ref[...] = jnp.zeros(carry_ref.shape, F32)

    gate_ref[...] = _dot(hm_ref[...], wg_ref[...]).astype(BF16)
    left = (cw_ref.shape[0] - 1) // 2
    xc = _dwconv(_dot(hh_ref[...], wx_ref[...]), cw_ref[...], cb_ref[...], tile, left)
    xc_bf = xc.astype(BF16)
    xc_ref[...] = xc_bf
    a, b = _lru_coeffs(xc, xc_bf, gw_ref, gab_ref, gxb_ref, lam_ref)
    a_ref[...] = a
    b_ref[...] = b
    _scan_tile(a_ref, b_ref, carry_ref, reverse=False)
    hf_ref[...] = b_ref[...].astype(BF16)


def _rec_fwd(h, w_g, w_x, conv_w, conv_b, gw, gab, gxb, lam, *, tile, geom):
    n, d = h.shape
    d_rnn = w_x.shape[1]
    tok = pl.BlockSpec((tile, d_rnn), lambda i: (i, 0))
    in_specs = _halo_specs(tile, d, n) + [_const_spec(a.shape) for a in
                                          (w_g, w_x, conv_w, conv_b, gw, gab, gxb, lam)]
    out = jax.ShapeDtypeStruct((n, d_rnn), BF16)
    return pl.pallas_call(
        functools.partial(_rec_fwd_kernel, tile=tile, geom=geom),
        grid=(n // tile,),
        in_specs=in_specs,
        out_specs=[tok, tok, tok],
        out_shape=[out, out, out],
        scratch_shapes=[pltpu.VMEM((tile + 2 * HALO, d), BF16), pltpu.VMEM((tile, d_rnn), F32),
                        pltpu.VMEM((tile, d_rnn), F32), pltpu.VMEM((SUB, d_rnn), F32)],
        compiler_params=_params(("arbitrary",)),
        name="rglru_fwd",
    )(h, h, h, w_g, w_x, conv_w, conv_b, gw, gab, gxb, lam)


def _rec_bwd_kernel(xc_ref, gate_ref, hf_ref, x_ref, gw_ref, gab_ref, gxb_ref, lam_ref, wo_ref,
                    gpost_ref, gnext_ref, xo_ref, ho_ref, a_ref, b_ref, carry_ref, *, tile, geom):
    i = pl.num_programs(0) - 1 - pl.program_id(0)
    pos0, seq = _seq_pos(i * tile, *geom)

    @pl.when(pos0 + tile == seq)
    def _():
        carry_ref[...] = jnp.zeros(carry_ref.shape, F32)

    xc_bf = xc_ref[...]
    a, b = _lru_coeffs(xc_bf.astype(F32), xc_bf, gw_ref, gab_ref, gxb_ref, lam_ref)
    a_ref[...] = a
    b_ref[...] = b
    _scan_tile(a_ref, b_ref, carry_ref, reverse=True)
    y = (hf_ref[...].astype(F32) + b_ref[...]) * _gelu_tanh(gate_ref[...].astype(F32))
    m = _dot(y.astype(BF16), wo_ref[...])
    xo = x_ref[...] + _rms(m, gpost_ref[...])
    xo_ref[...] = xo
    ho_ref[...] = _rms(xo, gnext_ref[...]).astype(BF16)


def _rec_bwd(xc, gate, hf, x, gw, gab, gxb, lam, w_out, g_post, g_next, *, tile, geom):
    n, d = x.shape
    d_rnn = xc.shape[1]
    nt = n // tile
    rtok = lambda w: pl.BlockSpec((tile, w), lambda i: (nt - 1 - i, 0))
    in_specs = [rtok(d_rnn), rtok(d_rnn), rtok(d_rnn), rtok(d)] + [
        _const_spec(a.shape) for a in (gw, gab, gxb, lam, w_out, g_post, g_next)]
    return pl.pallas_call(
        functools.partial(_rec_bwd_kernel, tile=tile, geom=geom),
        grid=(nt,),
        in_specs=in_specs,
        out_specs=[rtok(d), rtok(d)],
        out_shape=[jax.ShapeDtypeStruct((n, d), F32), jax.ShapeDtypeStruct((n, d), BF16)],
        scratch_shapes=[pltpu.VMEM((tile, d_rnn), F32), pltpu.VMEM((tile, d_rnn), F32),
                        pltpu.VMEM((SUB, d_rnn), F32)],
        compiler_params=_params(("arbitrary",)),
        name="rglru_bwd",
    )(xc, gate, hf, x, gw, gab, gxb, lam, w_out, g_post, g_next)


def _prenorm_kernel(x_ref, g_ref, h_ref):
    h_ref[...] = _rms(x_ref[...], g_ref[...]).astype(BF16)


def _prenorm(x, g, *, tile):
    n, d = x.shape
    tok = pl.BlockSpec((tile, d), lambda i: (i, 0))
    return pl.pallas_call(
        _prenorm_kernel, grid=(n // tile,), in_specs=[tok, _const_spec(g.shape)], out_specs=tok,
        out_shape=jax.ShapeDtypeStruct((n, d), BF16), compiler_params=_params(("parallel",)),
        name="prenorm",
    )(x, g)


def _row(v):
    return v.reshape(1, -1).astype(F32)


def _rope_tables(s_max):
    half = MLA_ROPE // 2
    pos = jnp.arange(s_max, dtype=F32)
    inv = ROPE_BASE ** (-jnp.arange(0, MLA_ROPE, 2, dtype=F32) / MLA_ROPE)
    ang = pos[:, None] * inv[None, :]
    cos, sin = jnp.cos(ang), jnp.sin(ang)
    pad = LANES - MLA_NOPE - MLA_ROPE
    cos_t = jnp.concatenate([jnp.ones((s_max, MLA_NOPE), F32), cos, cos, jnp.zeros((s_max, pad), F32)], axis=1)
    sin_t = jnp.concatenate([jnp.zeros((s_max, MLA_NOPE), F32), -sin, sin, jnp.zeros((s_max, pad), F32)], axis=1)
    del half
    return cos_t, sin_t


def _swap_halves(w):
    half = w.shape[-1] // 2
    return jnp.concatenate([w[..., half:], w[..., :half]], axis=-1)


def _attn_weights(w_in, w_uq, w_ukv):
    d = w_in.shape[0]
    q_lora = w_uq.shape[0]
    kv_lora = w_ukv.shape[0]
    o1 = q_lora + kv_lora
    o2 = o1 + MLA_ROPE
    pad = LANES - MLA_NOPE - MLA_ROPE
    kpe = w_in[:, o1:o2]
    zl = jnp.zeros((d, MLA_NOPE), w_in.dtype)
    zr = jnp.zeros((d, pad), w_in.dtype)
    w_all = jnp.concatenate([w_in[:, :o1], zl, kpe, zr, zl, _swap_halves(kpe), zr, w_in[:, o2:]], axis=1)

    wq = w_uq.reshape(q_lora, MLA_HEADS, MLA_NOPE + MLA_ROPE)
    zq = jnp.zeros((q_lora, MLA_HEADS, pad), w_uq.dtype)
    wq_p = jnp.concatenate([wq, zq], axis=-1).reshape(q_lora, MLA_HEADS * LANES)
    wq_s = jnp.concatenate([jnp.zeros_like(wq[..., :MLA_NOPE]), _swap_halves(wq[..., MLA_NOPE:]), zq],
                           axis=-1).reshape(q_lora, MLA_HEADS * LANES)

    wkv = w_ukv.reshape(kv_lora, MLA_HEADS, MLA_NOPE + MLA_V)
    zk = jnp.zeros((kv_lora, MLA_HEADS, LANES - MLA_NOPE), w_ukv.dtype)
    wk_p = jnp.concatenate([wkv[..., :MLA_NOPE], zk], axis=-1).reshape(kv_lora, MLA_HEADS * LANES)
    wv = wkv[..., MLA_NOPE:].reshape(kv_lora, MLA_HEADS * MLA_V)
    return tuple(a.astype(BF16) for a in (w_all, wq_p, wq_s, wk_p, wv))


TOKEN_TILE = 512
FFN_TILE = 1024
FFN_CHUNK = 1024
REC_TILE = 512
MLA_TQ = 256
MLA_KC = 512


def kernel(x_prompt, x_sample, norm_mix_pre, norm_mix_post, norm_ffn_pre, norm_ffn_post, w_in_attn, q_norm, w_uq, kv_norm, w_ukv, na_rpb, w_out_attn, w_in_rec, conv_w_rec, conv_b_rec, gate_a_w, gate_a_b, gate_x_w, gate_x_b, lru_lambda, w_out_rec, w_ffn_up, conv_w_ffn, conv_b_ffn, w_ffn_down):
    bp, s_p, d = x_prompt.shape
    bs, s_s, _ = x_sample.shape
    n_p = bp * s_p
    n_s = bs * s_s
    geom = (n_p, s_p, s_s)
    depth = norm_mix_pre.shape[0]
    x = jnp.concatenate([x_prompt.reshape(n_p, d), x_sample.reshape(n_s, d)], axis=0)
    cos_t, sin_t = _rope_tables(max(s_p, s_s))

    tile = min(TOKEN_TILE, s_p, s_s)
    ffn_tile = min(FFN_TILE, s_p, s_s)
    rec_tile = min(REC_TILE, s_p, s_s)
    h = None
    for li in range(depth):
        j = li // 2
        last = li == depth - 1
        g_after_ffn = _row(norm_mix_pre[li + 1]) if not last else _row(norm_mix_pre[li])
        if li % 2 == 0:
            w_all, wq_p, wq_s, wk_p, wv = _attn_weights(w_in_attn[j], w_uq[j], w_ukv[j])
            q, k, v, naq, nak, nav = _attn_in(
                x, _row(norm_mix_pre[li]), w_all, _row(q_norm[j]), wq_p, wq_s, _row(kv_norm[j]), wk_p, wv,
                cos_t, sin_t, tile=tile, geom=geom)
            tq_p, tq_s = min(MLA_TQ, s_p), min(MLA_TQ, s_s)
            o_mla = _mla_group(q, k, v, None, tok_base=0, batch=bp, seq=s_p, tq=tq_p, kc=min(MLA_KC, s_p))
            o_mla = _mla_group(q, k, v, o_mla, tok_base=n_p, batch=bs, seq=s_s, tq=tq_s, kc=min(MLA_KC, s_s))
            bias = _na_bias_table(na_rpb[j])
            o_na = _na_group(naq, nak, nav, bias, None, tok_base=0, batch=bp, seq=s_p)
            o_na = _na_group(naq, nak, nav, bias, o_na, tok_base=n_p, batch=bs, seq=s_s)
            w_o = w_out_attn[j].astype(BF16)
            n_mla = MLA_HEADS * MLA_V
            x, h = _attn_out(o_mla, o_na, w_o[:n_mla], w_o[n_mla:], x, _row(norm_mix_post[li]),
                             _row(norm_ffn_pre[li]), tile=tile)
        else:
            if h is None:
                h = _prenorm(x, _row(norm_mix_pre[li]), tile=tile)
            d_rnn = w_out_rec.shape[1]
            w_in = w_in_rec[j].astype(BF16)
            gws = [jnp.concatenate([gate_a_w[j, e], gate_x_w[j, e]], axis=-1).astype(BF16) for e in range(2)]
            gate, xc, hf = _rec_fwd(h, w_in[:, :d_rnn], w_in[:, d_rnn:], conv_w_rec[j], _row(conv_b_rec[j]),
                                    gws[0], _row(gate_a_b[j, 0]), _row(gate_x_b[j, 0]), _row(lru_lambda[j, 0]),
                                    tile=rec_tile, geom=geom)
            x, h = _rec_bwd(xc, gate, hf, x, gws[1], _row(gate_a_b[j, 1]), _row(gate_x_b[j, 1]),
                            _row(lru_lambda[j, 1]), w_out_rec[j].astype(BF16), _row(norm_mix_post[li]),
                            _row(norm_ffn_pre[li]), tile=rec_tile, geom=geom)
        x, h = _ffn(h, x, w_ffn_up[li].astype(BF16), conv_w_ffn[li], _row(conv_b_ffn[li]),
                    w_ffn_down[li].astype(BF16), _row(norm_ffn_post[li]), g_after_ffn,
                    tile=ffn_tile, fc=min(FFN_CHUNK, w_ffn_down.shape[1]), geom=geom)
    return x[:n_p].reshape(bp, s_p, d), x[n_p:].reshape(bs, s_s, d)
```

```python
import functools
import math

import numpy as np
import jax
import jax.numpy as jnp
from jax import lax
from jax.experimental import pallas as pl
from jax.experimental.pallas import tpu as pltpu

F32 = jnp.float32
BF16 = jnp.bfloat16

EPS = 1e-6
GRID_W = 64
MLA_HEADS = 8
MLA_NOPE = 64
MLA_ROPE = 32
MLA_V = 64
ROPE_BASE = 10000.0
NA_HEADS = 8
NA_HD = 64
NA_KH = 8
NA_KW = 16
LRU_C = 8.0

LANES = 128
HALO = 16
NEG = -1e30
VMEM_LIMIT = 56 * 1024 * 1024


def _rms(x, g):
    return x * lax.rsqrt(jnp.mean(x * x, axis=-1, keepdims=True) + EPS) * g


def _gelu_tanh(x):
    return 0.5 * x * (1.0 + jnp.tanh(math.sqrt(2.0 / math.pi) * (x + 0.044715 * (x * x * x))))


def _dot(a, b):
    return jnp.dot(a, b, preferred_element_type=F32)


def _dot_nt(a, b):
    return lax.dot_general(a, b, (((1,), (1,)), ((), ())), preferred_element_type=F32)


def _seq_pos(tok0, n_prompt, s_prompt, s_sample):
    in_p = tok0 < n_prompt
    seq = jnp.where(in_p, s_prompt, s_sample)
    rel = jnp.where(in_p, tok0, tok0 - n_prompt)
    return lax.rem(rel, seq), seq


def _params(sem):
    return pltpu.CompilerParams(dimension_semantics=sem, vmem_limit_bytes=VMEM_LIMIT)


def _const_spec(shape):
    nd = len(shape)
    return pl.BlockSpec(shape, lambda *_: (0,) * nd)


def _attn_in_kernel(x_ref, g_ref, win_ref, qn_ref, wq_ref, wqs_ref, kvn_ref, wk_ref, wv_ref,
                    cos_ref, sin_ref,
                    q_ref, k_ref, v_ref, naq_ref, nak_ref, nav_ref, *, q_lora, kv_lora, q_scale, na_scale):
    h = _rms(x_ref[...], g_ref[...]).astype(BF16)
    z = _dot(h, win_ref[...])
    o0 = q_lora
    o1 = o0 + kv_lora
    o2 = o1 + LANES
    o3 = o2 + LANES
    na_w = NA_HEADS * NA_HD
    cos = cos_ref[...]
    sin = sin_ref[...]

    nq = _rms(z[:, :o0], qn_ref[...]).astype(BF16)
    q = _dot(nq, wq_ref[...])
    qs = _dot(nq, wqs_ref[...])
    for hd in range(MLA_HEADS):
        sl = slice(hd * LANES, (hd + 1) * LANES)
        q_ref[:, sl] = ((q[:, sl] * cos + qs[:, sl] * sin) * q_scale).astype(BF16)

    nkv = _rms(z[:, o0:o1], kvn_ref[...]).astype(BF16)
    k = _dot(nkv, wk_ref[...])
    rot = z[:, o1:o2] * cos + z[:, o2:o3] * sin
    for hd in range(MLA_HEADS):
        sl = slice(hd * LANES, (hd + 1) * LANES)
        k_ref[:, sl] = (k[:, sl] + rot).astype(BF16)
    v_ref[...] = _dot(nkv, wv_ref[...]).astype(BF16)

    naq_ref[...] = (z[:, o3:o3 + na_w] * na_scale).astype(BF16)
    nak_ref[...] = z[:, o3 + na_w:o3 + 2 * na_w].astype(BF16)
    nav_ref[...] = z[:, o3 + 2 * na_w:o3 + 3 * na_w].astype(BF16)


def _attn_in(x, g, w_all, qn, wq, wqs, kvn, wk, wv, cos_t, sin_t, *, tile, geom):
    n, d = x.shape
    n_p, s_p, s_s = geom
    q_lora = qn.shape[1]
    kv_lora = kvn.shape[1]
    hw = MLA_HEADS * LANES
    na_w = NA_HEADS * NA_HD

    def rope_idx(i):
        pos0, _ = _seq_pos(i * tile, n_p, s_p, s_s)
        return (pos0 // tile, 0)

    tok = lambda w: pl.BlockSpec((tile, w), lambda i: (i, 0))
    kern = functools.partial(_attn_in_kernel, q_lora=q_lora, kv_lora=kv_lora,
                             q_scale=float((MLA_NOPE + MLA_ROPE) ** -0.5 * math.log2(math.e)),
                             na_scale=float(NA_HD ** -0.5 * math.log2(math.e)))
    return pl.pallas_call(
        kern,
        grid=(n // tile,),
        in_specs=[tok(d), _const_spec(g.shape), _const_spec(w_all.shape), _const_spec(qn.shape),
                  _const_spec(wq.shape), _const_spec(wqs.shape), _const_spec(kvn.shape),
                  _const_spec(wk.shape), _const_spec(wv.shape),
                  pl.BlockSpec((tile, LANES), rope_idx), pl.BlockSpec((tile, LANES), rope_idx)],
        out_specs=[tok(hw), tok(hw), tok(MLA_HEADS * MLA_V), tok(na_w), tok(na_w), tok(na_w)],
        out_shape=[jax.ShapeDtypeStruct((n, hw), BF16), jax.ShapeDtypeStruct((n, hw), BF16),
                   jax.ShapeDtypeStruct((n, MLA_HEADS * MLA_V), BF16),
                   jax.ShapeDtypeStruct((n, na_w), BF16), jax.ShapeDtypeStruct((n, na_w), BF16),
                   jax.ShapeDtypeStruct((n, na_w), BF16)],
        compiler_params=_params(("parallel",)),
        name="attn_in",
    )(x, g, w_all, qn, wq, wqs, kvn, wk, wv, cos_t, sin_t)


def _mla_kernel(*refs, aliased):
    if aliased:
        q_ref, k_ref, v_ref, _, o_ref = refs
    else:
        q_ref, k_ref, v_ref, o_ref = refs
    tq = q_ref.shape[0]
    seq = k_ref.shape[0]
    heads = [slice(hd * LANES, (hd + 1) * LANES) for hd in range(2)]
    scores = [_dot_nt(q_ref[:, hsl], k_ref[:, hsl]) for hsl in heads]
    probs = [jnp.exp2((s - jnp.max(s, axis=-1, keepdims=True)).astype(BF16)) for s in scores]
    v_ext = jnp.concatenate([v_ref[...], jnp.ones((seq, LANES), BF16)], axis=1)
    accs = [_dot(p, v_ext) for p in probs]
    outs = [acc[:, :LANES] / acc[:, LANES:LANES + 1] for acc in accs]
    lane = lax.broadcasted_iota(jnp.int32, (tq, LANES), 1)
    o_ref[...] = jnp.where(lane < MLA_V, outs[0], outs[1]).astype(BF16)


def _mla_group(q, k, v, o_prev, *, tok_base, batch, seq, tq):
    n = q.shape[0]
    qb = tok_base // tq
    sb = tok_base // seq
    nq = seq // tq
    in_specs = [pl.BlockSpec((tq, 2 * LANES), lambda b, hp, i: (qb + b * nq + i, hp)),
                pl.BlockSpec((seq, 2 * LANES), lambda b, hp, i: (sb + b, hp)),
                pl.BlockSpec((seq, LANES), lambda b, hp, i: (sb + b, hp))]
    args = [q, k, v]
    aliases = {}
    if o_prev is not None:
        in_specs.append(pl.BlockSpec(memory_space=pl.ANY))
        args.append(o_prev)
        aliases = {3: 0}
    return pl.pallas_call(
        functools.partial(_mla_kernel, aliased=o_prev is not None),
        grid=(batch, MLA_HEADS // 2, nq),
        in_specs=in_specs,
        out_specs=pl.BlockSpec((tq, LANES), lambda b, hp, i: (qb + b * nq + i, hp)),
        out_shape=jax.ShapeDtypeStruct((n, MLA_HEADS * MLA_V), BF16),
        input_output_aliases=aliases,
        compiler_params=_params(("parallel", "parallel", "arbitrary")),
        name="mla_attn",
    )(*args)


NA_GROUP = 4


def _na_kernel(*refs, rows, aliased):
    if aliased:
        q_ref, k_ref, v_ref, bias_ref, _, o_ref = refs
    else:
        q_ref, k_ref, v_ref, bias_ref, o_ref = refs
    win = NA_KH * GRID_W
    lane_q = lax.broadcasted_iota(jnp.int32, (GRID_W, LANES), 1)
    ones = jnp.ones((win, LANES), BF16)

    def body(g, _):
        q0s, k0s, scores = [], [], []
        for j in range(NA_GROUP):
            r = g * NA_GROUP + j
            rs = jnp.clip(r - NA_KH // 2, 0, rows - NA_KH)
            q0 = pl.multiple_of(r * GRID_W, GRID_W)
            k0 = pl.multiple_of(rs * GRID_W, GRID_W)
            qr = q_ref[pl.ds(q0, GRID_W), :]
            zero = jnp.zeros_like(qr)
            q2 = jnp.concatenate([jnp.where(lane_q < NA_HD, qr, zero), jnp.where(lane_q >= NA_HD, qr, zero)],
                                 axis=0)
            scores.append(_dot_nt(q2, k_ref[pl.ds(k0, win), :]) + bias_ref[r - rs])
            q0s.append(q0)
            k0s.append(k0)
        probs = [jnp.exp2((s - jnp.max(s, axis=-1, keepdims=True)).astype(BF16)) for s in scores]
        accs = [_dot(p, jnp.concatenate([v_ref[pl.ds(k0, win), :], ones], axis=1)) for p, k0 in zip(probs, k0s)]
        for q0, acc in zip(q0s, accs):
            o2 = acc[:, :LANES] / acc[:, LANES:LANES + 1]
            o_ref[pl.ds(q0, GRID_W), :] = jnp.where(lane_q < NA_HD, o2[:GRID_W], o2[GRID_W:]).astype(BF16)
        return 0

    lax.fori_loop(0, rows // NA_GROUP, body, 0)


def _na_group(q, k, v, bias, o_prev, *, tok_base, batch, seq):
    n = q.shape[0]
    sb = tok_base // seq
    rows = seq // GRID_W
    blk = pl.BlockSpec((seq, LANES), lambda b, hp: (sb + b, hp))
    in_specs = [blk, blk, blk,
                pl.BlockSpec((None,) + bias.shape[1:], lambda b, hp: (hp, 0, 0, 0))]
    args = [q, k, v, bias]
    aliases = {}
    if o_prev is not None:
        in_specs.append(pl.BlockSpec(memory_space=pl.ANY))
        args.append(o_prev)
        aliases = {4: 0}
    return pl.pallas_call(
        functools.partial(_na_kernel, rows=rows, aliased=o_prev is not None),
        grid=(batch, NA_HEADS // 2),
        in_specs=in_specs,
        out_specs=blk,
        out_shape=jax.ShapeDtypeStruct((n, NA_HEADS * NA_HD), BF16),
        input_output_aliases=aliases,
        compiler_params=_params(("parallel", "parallel")),
        name="na_attn",
    )(*args)


def _na_bias_table(rpb):
    n_rel = 2 * NA_KW - 1
    period = 2 * GRID_W
    by_row = jnp.stack([rpb[:, NA_KH - 1 - dl:2 * NA_KH - 1 - dl, :] for dl in range(NA_KH)], axis=1)
    left = GRID_W - NA_KW
    p = jnp.pad(by_row.astype(F32), ((0, 0), (0, 0), (0, 0), (left, period - n_rel - left)))
    lead = p.shape[:-1]
    flat = jnp.broadcast_to(p[..., None, :], lead + (GRID_W, period)).reshape(lead + (GRID_W * period,))
    skew = flat[..., :GRID_W * (period - 1)].reshape(lead + (GRID_W, period - 1))
    t = skew[..., GRID_W - 1:2 * GRID_W - 1]
    cols = np.arange(GRID_W)
    col_start = np.clip(cols - NA_KW // 2, 0, GRID_W - NA_KW)
    kc = np.arange(GRID_W)
    in_win = (kc[None, :] >= col_start[:, None]) & (kc[None, :] < col_start[:, None] + NA_KW)
    t = jnp.where(jnp.asarray(in_win)[None, None, None], t * math.log2(math.e), NEG)
    t = jnp.swapaxes(t, 2, 3).reshape(rpb.shape[0] // 2, 2, NA_KH, GRID_W, NA_KH * GRID_W)
    return jnp.swapaxes(t, 1, 2).reshape(rpb.shape[0] // 2, NA_KH, 2 * GRID_W, NA_KH * GRID_W)


def _attn_out_kernel(om_ref, on_ref, wm_ref, wn_ref, x_ref, gpost_ref, gnext_ref, xo_ref, ho_ref):
    m = _dot(om_ref[...], wm_ref[...]) + _dot(on_ref[...], wn_ref[...])
    xo = x_ref[...] + _rms(m, gpost_ref[...])
    xo_ref[...] = xo
    ho_ref[...] = _rms(xo, gnext_ref[...]).astype(BF16)


def _attn_out(o_mla, o_na, w_m, w_n, x, g_post, g_next, *, tile):
    n, d = x.shape
    tok = lambda w: pl.BlockSpec((tile, w), lambda i: (i, 0))
    return pl.pallas_call(
        _attn_out_kernel,
        grid=(n // tile,),
        in_specs=[tok(o_mla.shape[1]), tok(o_na.shape[1]), _const_spec(w_m.shape), _const_spec(w_n.shape),
                  tok(d), _const_spec(g_post.shape), _const_spec(g_next.shape)],
        out_specs=[tok(d), tok(d)],
        out_shape=[jax.ShapeDtypeStruct((n, d), F32), jax.ShapeDtypeStruct((n, d), BF16)],
        compiler_params=_params(("parallel",)),
        name="attn_out",
    )(o_mla, o_na, w_m, w_n, x, g_post, g_next)


def _halo_specs(tile, d, n):
    per = tile // HALO
    last = n // HALO - 1
    return [pl.BlockSpec((HALO, d), lambda i, *_: (jnp.maximum(i * per - 1, 0), 0)),
            pl.BlockSpec((tile, d), lambda i, *_: (i, 0)),
            pl.BlockSpec((HALO, d), lambda i, *_: (jnp.minimum((i + 1) * per, last), 0))]


def _fill_halo(hh_ref, hp_ref, hm_ref, hn_ref, tok0, tile, geom):
    pos0, seq = _seq_pos(tok0, *geom)
    zero = jnp.zeros(hp_ref.shape, hp_ref.dtype)
    hh_ref[0:HALO, :] = jnp.where(pos0 == 0, zero, hp_ref[...])
    hh_ref[HALO:HALO + tile, :] = hm_ref[...]
    hh_ref[HALO + tile:, :] = jnp.where(pos0 + tile == seq, zero, hn_ref[...])


def _dwconv(u, w, b, tile, left):
    y = b
    for kk in range(w.shape[0]):
        s = HALO - left + kk
        y = y + w[kk:kk + 1, :] * u[s:s + tile, :]
    return y


def _ffn_kernel(*refs, tile, geom, emit_next):
    if emit_next:
        (hp_ref, hm_ref, hn_ref, x_ref, wg_ref, wv_ref, cwg_ref, cwv_ref, cbg_ref, cbv_ref, wd_ref,
         gpost_ref, gnext_ref, xo_ref, ho_ref, hh_ref, acc_ref) = refs
    else:
        (hp_ref, hm_ref, hn_ref, x_ref, wg_ref, wv_ref, cwg_ref, cwv_ref, cbg_ref, cbv_ref, wd_ref,
         gpost_ref, xo_ref, hh_ref, acc_ref) = refs
    i = pl.program_id(0)
    c = pl.program_id(1)

    @pl.when(c == 0)
    def _():
        _fill_halo(hh_ref, hp_ref, hm_ref, hn_ref, i * tile, tile, geom)
        acc_ref[...] = jnp.zeros(acc_ref.shape, F32)

    hh = hh_ref[...]
    left = (cwg_ref.shape[0] - 1) // 2
    g = _dwconv(_dot(hh, wg_ref[...]), cwg_ref[...], cbg_ref[...], tile, left)
    val = _dwconv(_dot(hh, wv_ref[...]), cwv_ref[...], cbv_ref[...], tile, left)
    a = (_gelu_tanh(g) * val).astype(BF16)
    acc_ref[...] += _dot(a, wd_ref[...])

    @pl.when(c == pl.num_programs(1) - 1)
    def _():
        xo = x_ref[...] + _rms(acc_ref[...], gpost_ref[...])
        xo_ref[...] = xo
        if emit_next:
            ho_ref[...] = _rms(xo, gnext_ref[...]).astype(BF16)


def _ffn(h, x, w_up, conv_w, conv_b, w_down, g_post, g_next, *, tile, fc, geom):
    n, d = x.shape
    d_ff = w_down.shape[0]
    nc = d_ff // fc
    kk = conv_w.shape[0]
    emit_next = g_next is not None
    tok = pl.BlockSpec((tile, d), lambda i, c: (i, 0))
    in_specs = _halo_specs(tile, d, n) + [
        tok,
        pl.BlockSpec((d, fc), lambda i, c: (0, c)),
        pl.BlockSpec((d, fc), lambda i, c: (0, nc + c)),
        pl.BlockSpec((kk, fc), lambda i, c: (0, c)),
        pl.BlockSpec((kk, fc), lambda i, c: (0, nc + c)),
        pl.BlockSpec((1, fc), lambda i, c: (0, c)),
        pl.BlockSpec((1, fc), lambda i, c: (0, nc + c)),
        pl.BlockSpec((fc, d), lambda i, c: (c, 0)),
        _const_spec(g_post.shape)]
    args = [h, h, h, x, w_up, w_up, conv_w, conv_w, conv_b, conv_b, w_down, g_post]
    out_specs = [tok]
    out_shape = [jax.ShapeDtypeStruct((n, d), F32)]
    if emit_next:
        in_specs.append(_const_spec(g_next.shape))
        args.append(g_next)
        out_specs.append(tok)
        out_shape.append(jax.ShapeDtypeStruct((n, d), BF16))
    res = pl.pallas_call(
        functools.partial(_ffn_kernel, tile=tile, geom=geom, emit_next=emit_next),
        grid=(n // tile, nc),
        in_specs=in_specs,
        out_specs=out_specs,
        out_shape=out_shape,
        scratch_shapes=[pltpu.VMEM((tile + 2 * HALO, d), BF16), pltpu.VMEM((tile, d), F32)],
        compiler_params=_params(("parallel", "arbitrary")),
        name="conv_ffn",
    )(*args)
    return (res[0], res[1]) if emit_next else (res[0], None)


SUB = 8


def _lru_coeffs(xc, xc_bf, gw_ref, gab_ref, gxb_ref, lam_ref):
    nb, bw, _ = gw_ref.shape
    lam = lam_ref[...]
    neg = -lam
    sp = jnp.maximum(neg, 0.0) + jnp.log1p(jnp.exp(-jnp.abs(neg)))
    a_parts, b_parts = [], []
    for blk in range(nb):
        sl = slice(blk * bw, (blk + 1) * bw)
        rg = _dot(xc_bf[:, sl], gw_ref[blk])
        r = jax.nn.sigmoid(rg[:, :bw] + gab_ref[:, sl])
        gi = jax.nn.sigmoid(rg[:, bw:] + gxb_ref[:, sl])
        log_a = (-LRU_C) * r * sp[:, sl]
        t = jnp.tanh(log_a)
        one_minus_a2 = (-2.0 * t) / (1.0 - t)
        a_parts.append(jnp.exp(log_a))
        b_parts.append(jnp.sqrt(one_minus_a2) * (gi * xc[:, sl]))
    return jnp.concatenate(a_parts, axis=1), jnp.concatenate(b_parts, axis=1)


def _scan_tile(a_ref, b_ref, carry_ref, *, reverse):
    t, ch = a_ref.shape
    n_chunks = t // SUB
    row = lax.broadcasted_iota(jnp.int32, (SUB, ch), 0)

    def body(j, carry):
        jj = (n_chunks - 1 - j) if reverse else j
        r0 = pl.multiple_of(jj * SUB, SUB)
        a = a_ref[pl.ds(r0, SUB), :]
        b = b_ref[pl.ds(r0, SUB), :]
        for dist in (1, 2, 4):
            shift = (SUB - dist) if reverse else dist
            a_sh = pltpu.roll(a, shift, axis=0)
            b_sh = pltpu.roll(b, shift, axis=0)
            valid = (row < SUB - dist) if reverse else (row >= dist)
            b = jnp.where(valid, a * b_sh + b, b)
            a = jnp.where(valid, a * a_sh, a)
        hcur = a * carry + b
        b_ref[pl.ds(r0, SUB), :] = hcur
        edge = hcur[0:1, :] if reverse else hcur[SUB - 1:SUB, :]
        return jnp.broadcast_to(edge, (SUB, ch))

    carry_ref[...] = lax.fori_loop(0, n_chunks, body, carry_ref[...])


def _rec_fwd_kernel(hp_ref, hm_ref, hn_ref, wg_ref, wx_ref, cw_ref, cb_ref, gw_ref, gab_ref, gxb_ref,
                    lam_ref, gate_ref, xc_ref, hf_ref, hh_ref, a_ref, b_ref, carry_ref, *, tile, geom):
    i = pl.program_id(0)
    pos0, _ = _seq_pos(i * tile, *geom)
    _fill_halo(hh_ref, hp_ref, hm_ref, hn_ref, i * tile, tile, geom)

    @pl.when(pos0 == 0)
    def _():
        carry_ref[...] = jnp.zeros(carry_ref.shape, F32)

    gate_ref[...] = _dot(hm_ref[...], wg_ref[...]).astype(BF16)
    left = (cw_ref.shape[0] - 1) // 2
    xc = _dwconv(_dot(hh_ref[...], wx_ref[...]), cw_ref[...], cb_ref[...], tile, left)
    xc_bf = xc.astype(BF16)
    xc_ref[...] = xc_bf
    a, b = _lru_coeffs(xc, xc_bf, gw_ref, gab_ref, gxb_ref, lam_ref)
    a_ref[...] = a
    b_ref[...] = b
    _scan_tile(a_ref, b_ref, carry_ref, reverse=False)
    hf_ref[...] = b_ref[...].astype(BF16)


def _rec_fwd(h, w_g, w_x, conv_w, conv_b, gw, gab, gxb, lam, *, tile, geom):
    n, d = h.shape
    d_rnn = w_x.shape[1]
    tok = pl.BlockSpec((tile, d_rnn), lambda i: (i, 0))
    in_specs = _halo_specs(tile, d, n) + [_const_spec(a.shape) for a in
                                          (w_g, w_x, conv_w, conv_b, gw, gab, gxb, lam)]
    out = jax.ShapeDtypeStruct((n, d_rnn), BF16)
    return pl.pallas_call(
        functools.partial(_rec_fwd_kernel, tile=tile, geom=geom),
        grid=(n // tile,),
        in_specs=in_specs,
        out_specs=[tok, tok, tok],
        out_shape=[out, out, out],
        scratch_shapes=[pltpu.VMEM((tile + 2 * HALO, d), BF16), pltpu.VMEM((tile, d_rnn), F32),
                        pltpu.VMEM((tile, d_rnn), F32), pltpu.VMEM((SUB, d_rnn), F32)],
        compiler_params=_params(("arbitrary",)),
        name="rglru_fwd",
    )(h, h, h, w_g, w_x, conv_w, conv_b, gw, gab, gxb, lam)


def _rec_bwd_kernel(xc_ref, gate_ref, hf_ref, x_ref, gw_ref, gab_ref, gxb_ref, lam_ref, wo_ref,
                    gpost_ref, gnext_ref, xo_ref, ho_ref, a_ref, b_ref, carry_ref, *, tile, geom):
    i = pl.num_programs(0) - 1 - pl.program_id(0)
    pos0, seq = _seq_pos(i * tile, *geom)

    @pl.when(pos0 + tile == seq)
    def _():
        carry_ref[...] = jnp.zeros(carry_ref.shape, F32)

    xc_bf = xc_ref[...]
    a, b = _lru_coeffs(xc_bf.astype(F32), xc_bf, gw_ref, gab_ref, gxb_ref, lam_ref)
    a_ref[...] = a
    b_ref[...] = b
    _scan_tile(a_ref, b_ref, carry_ref, reverse=True)
    y = (hf_ref[...].astype(F32) + b_ref[...]) * _gelu_tanh(gate_ref[...].astype(F32))
    m = _dot(y.astype(BF16), wo_ref[...])
    xo = x_ref[...] + _rms(m, gpost_ref[...])
    xo_ref[...] = xo
    ho_ref[...] = _rms(xo, gnext_ref[...]).astype(BF16)


def _rec_bwd(xc, gate, hf, x, gw, gab, gxb, lam, w_out, g_post, g_next, *, tile, geom):
    n, d = x.shape
    d_rnn = xc.shape[1]
    nt = n // tile
    rtok = lambda w: pl.BlockSpec((tile, w), lambda i: (nt - 1 - i, 0))
    in_specs = [rtok(d_rnn), rtok(d_rnn), rtok(d_rnn), rtok(d)] + [
        _const_spec(a.shape) for a in (gw, gab, gxb, lam, w_out, g_post, g_next)]
    return pl.pallas_call(
        functools.partial(_rec_bwd_kernel, tile=tile, geom=geom),
        grid=(nt,),
        in_specs=in_specs,
        out_specs=[rtok(d), rtok(d)],
        out_shape=[jax.ShapeDtypeStruct((n, d), F32), jax.ShapeDtypeStruct((n, d), BF16)],
        scratch_shapes=[pltpu.VMEM((tile, d_rnn), F32), pltpu.VMEM((tile, d_rnn), F32),
                        pltpu.VMEM((SUB, d_rnn), F32)],
        compiler_params=_params(("arbitrary",)),
        name="rglru_bwd",
    )(xc, gate, hf, x, gw, gab, gxb, lam, w_out, g_post, g_next)


def _prenorm_kernel(x_ref, g_ref, h_ref):
    h_ref[...] = _rms(x_ref[...], g_ref[...]).astype(BF16)


def _prenorm(x, g, *, tile):
    n, d = x.shape
    tok = pl.BlockSpec((tile, d), lambda i: (i, 0))
    return pl.pallas_call(
        _prenorm_kernel, grid=(n // tile,), in_specs=[tok, _const_spec(g.shape)], out_specs=tok,
        out_shape=jax.ShapeDtypeStruct((n, d), BF16), compiler_params=_params(("parallel",)),
        name="prenorm",
    )(x, g)


def _row(v):
    return v.reshape(1, -1).astype(F32)


def _rope_tables(s_max):
    half = MLA_ROPE // 2
    pos = jnp.arange(s_max, dtype=F32)
    inv = ROPE_BASE ** (-jnp.arange(0, MLA_ROPE, 2, dtype=F32) / MLA_ROPE)
    ang = pos[:, None] * inv[None, :]
    cos, sin = jnp.cos(ang), jnp.sin(ang)
    pad = LANES - MLA_NOPE - MLA_ROPE
    cos_t = jnp.concatenate([jnp.ones((s_max, MLA_NOPE), F32), cos, cos, jnp.zeros((s_max, pad), F32)], axis=1)
    sin_t = jnp.concatenate([jnp.zeros((s_max, MLA_NOPE), F32), -sin, sin, jnp.zeros((s_max, pad), F32)], axis=1)
    del half
    return cos_t, sin_t


def _swap_halves(w):
    half = w.shape[-1] // 2
    return jnp.concatenate([w[..., half:], w[..., :half]], axis=-1)


def _attn_weights(w_in, w_uq, w_ukv):
    d = w_in.shape[0]
    q_lora = w_uq.shape[0]
    kv_lora = w_ukv.shape[0]
    o1 = q_lora + kv_lora
    o2 = o1 + MLA_ROPE
    pad = LANES - MLA_NOPE - MLA_ROPE
    kpe = w_in[:, o1:o2]
    zl = jnp.zeros((d, MLA_NOPE), w_in.dtype)
    zr = jnp.zeros((d, pad), w_in.dtype)
    w_all = jnp.concatenate([w_in[:, :o1], zl, kpe, zr, zl, _swap_halves(kpe), zr, w_in[:, o2:]], axis=1)

    wq = w_uq.reshape(q_lora, MLA_HEADS, MLA_NOPE + MLA_ROPE)
    zq = jnp.zeros((q_lora, MLA_HEADS, pad), w_uq.dtype)
    wq_p = jnp.concatenate([wq, zq], axis=-1).reshape(q_lora, MLA_HEADS * LANES)
    wq_s = jnp.concatenate([jnp.zeros_like(wq[..., :MLA_NOPE]), _swap_halves(wq[..., MLA_NOPE:]), zq],
                           axis=-1).reshape(q_lora, MLA_HEADS * LANES)

    wkv = w_ukv.reshape(kv_lora, MLA_HEADS, MLA_NOPE + MLA_V)
    zk = jnp.zeros((kv_lora, MLA_HEADS, LANES - MLA_NOPE), w_ukv.dtype)
    wk_p = jnp.concatenate([wkv[..., :MLA_NOPE], zk], axis=-1).reshape(kv_lora, MLA_HEADS * LANES)
    wv = wkv[..., MLA_NOPE:].reshape(kv_lora, MLA_HEADS * MLA_V)
    return tuple(a.astype(BF16) for a in (w_all, wq_p, wq_s, wk_p, wv))


TOKEN_TILE = 512
FFN_TILE = 1024
FFN_CHUNK = 1024
REC_TILE = 512
MLA_TQ = 256


def kernel(x_prompt, x_sample, norm_mix_pre, norm_mix_post, norm_ffn_pre, norm_ffn_post, w_in_attn, q_norm, w_uq, kv_norm, w_ukv, na_rpb, w_out_attn, w_in_rec, conv_w_rec, conv_b_rec, gate_a_w, gate_a_b, gate_x_w, gate_x_b, lru_lambda, w_out_rec, w_ffn_up, conv_w_ffn, conv_b_ffn, w_ffn_down):
    bp, s_p, d = x_prompt.shape
    bs, s_s, _ = x_sample.shape
    n_p = bp * s_p
    n_s = bs * s_s
    geom = (n_p, s_p, s_s)
    depth = norm_mix_pre.shape[0]
    x = jnp.concatenate([x_prompt.reshape(n_p, d), x_sample.reshape(n_s, d)], axis=0)
    cos_t, sin_t = _rope_tables(max(s_p, s_s))

    tile = min(TOKEN_TILE, s_p, s_s)
    ffn_tile = min(FFN_TILE, s_p, s_s)
    rec_tile = min(REC_TILE, s_p, s_s)
    h = None
    for li in range(depth):
        j = li // 2
        last = li == depth - 1
        g_after_ffn = _row(norm_mix_pre[li + 1]) if not last else None
        if li % 2 == 0:
            w_all, wq_p, wq_s, wk_p, wv = _attn_weights(w_in_attn[j], w_uq[j], w_ukv[j])
            q, k, v, naq, nak, nav = _attn_in(
                x, _row(norm_mix_pre[li]), w_all, _row(q_norm[j]), wq_p, wq_s, _row(kv_norm[j]), wk_p, wv,
                cos_t, sin_t, tile=tile, geom=geom)
            tq_p, tq_s = min(MLA_TQ, s_p), min(MLA_TQ, s_s)
            o_mla = _mla_group(q, k, v, None, tok_base=0, batch=bp, seq=s_p, tq=tq_p)
            o_mla = _mla_group(q, k, v, o_mla, tok_base=n_p, batch=bs, seq=s_s, tq=tq_s)
            bias = _na_bias_table(na_rpb[j])
            o_na = _na_group(naq, nak, nav, bias, None, tok_base=0, batch=bp, seq=s_p)
            o_na = _na_group(naq, nak, nav, bias, o_na, tok_base=n_p, batch=bs, seq=s_s)
            w_o = w_out_attn[j].astype(BF16)
            n_mla = MLA_HEADS * MLA_V
            x, h = _attn_out(o_mla, o_na, w_o[:n_mla], w_o[n_mla:], x, _row(norm_mix_post[li]),
                             _row(norm_ffn_pre[li]), tile=tile)
        else:
            if h is None:
                h = _prenorm(x, _row(norm_mix_pre[li]), tile=tile)
            d_rnn = w_out_rec.shape[1]
            w_in = w_in_rec[j].astype(BF16)
            gws = [jnp.concatenate([gate_a_w[j, e], gate_x_w[j, e]], axis=-1).astype(BF16) for e in range(2)]
            gate, xc, hf = _rec_fwd(h, w_in[:, :d_rnn], w_in[:, d_rnn:], conv_w_rec[j], _row(conv_b_rec[j]),
                                    gws[0], _row(gate_a_b[j, 0]), _row(gate_x_b[j, 0]), _row(lru_lambda[j, 0]),
                                    tile=rec_tile, geom=geom)
            x, h = _rec_bwd(xc, gate, hf, x, gws[1], _row(gate_a_b[j, 1]), _row(gate_x_b[j, 1]),
                            _row(lru_lambda[j, 1]), w_out_rec[j].astype(BF16), _row(norm_mix_post[li]),
                            _row(norm_ffn_pre[li]), tile=rec_tile, geom=geom)
        x, h = _ffn(h, x, w_ffn_up[li].astype(BF16), conv_w_ffn[li], _row(conv_b_ffn[li]),
                    w_ffn_down[li].astype(BF16), _row(norm_ffn_post[li]), g_after_ffn,
                    tile=ffn_tile, fc=min(FFN_CHUNK, w_ffn_down.shape[1]), geom=geom)
    return x[:n_p].reshape(bp, s_p, d), x[n_p:].reshape(bs, s_s, d)
```

```python
import functools
import math

import numpy as np
import jax
import jax.numpy as jnp
from jax import lax
from jax.experimental import pallas as pl
from jax.experimental.pallas import tpu as pltpu

F32 = jnp.float32
BF16 = jnp.bfloat16

EPS = 1e-6
GRID_W = 64
MLA_HEADS = 8
MLA_NOPE = 64
MLA_ROPE = 32
MLA_V = 64
ROPE_BASE = 10000.0
NA_HEADS = 8
NA_HD = 64
NA_KH = 8
NA_KW = 16
LRU_C = 8.0

LANES = 128
SUB = 8
HALO = 16
NEG = -1e30
VMEM_LIMIT = 56 * 1024 * 1024
LOG2E = math.log2(math.e)


def _rms(x, g):
    return x * lax.rsqrt(jnp.mean(x * x, axis=-1, keepdims=True) + EPS) * g


def _gelu_tanh(x):
    c1 = math.sqrt(2.0 / math.pi)
    k1 = -2.0 * c1 * LOG2E
    k2 = k1 * 0.044715
    e = jnp.exp2(x * ((x * x) * k2 + k1))
    return x / (1.0 + e)


def _dot(a, b):
    return jnp.dot(a, b, preferred_element_type=F32)


def _dot_nt(a, b):
    return lax.dot_general(a, b, (((1,), (1,)), ((), ())), preferred_element_type=F32)


def _params(sem):
    return pltpu.CompilerParams(dimension_semantics=sem, vmem_limit_bytes=VMEM_LIMIT)


def _const_spec(shape):
    nd = len(shape)
    return pl.BlockSpec(shape, lambda *_: (0,) * nd)


def _attn_in_kernel(x_ref, g_ref, win_ref, qn_ref, wq_ref, wqs_ref, kvn_ref, wk_ref, wv_ref,
                    cos_ref, sin_ref,
                    q_ref, k_ref, v_ref, naq_ref, nak_ref, nav_ref, *, q_lora, kv_lora, q_scale, na_scale):
    h = _rms(x_ref[...], g_ref[...]).astype(BF16)
    z = _dot(h, win_ref[...])
    o0 = q_lora
    o1 = o0 + kv_lora
    o2 = o1 + LANES
    o3 = o2 + LANES
    na_w = NA_HEADS * NA_HD
    cos = cos_ref[...]
    sin = sin_ref[...]

    nq = _rms(z[:, :o0], qn_ref[...]).astype(BF16)
    q = _dot(nq, wq_ref[...])
    qs = _dot(nq, wqs_ref[...])
    for hd in range(MLA_HEADS):
        sl = slice(hd * LANES, (hd + 1) * LANES)
        q_ref[:, sl] = ((q[:, sl] * cos + qs[:, sl] * sin) * q_scale).astype(BF16)

    nkv = _rms(z[:, o0:o1], kvn_ref[...]).astype(BF16)
    k = _dot(nkv, wk_ref[...])
    rot = z[:, o1:o2] * cos + z[:, o2:o3] * sin
    for hd in range(MLA_HEADS):
        sl = slice(hd * LANES, (hd + 1) * LANES)
        k_ref[:, sl] = (k[:, sl] + rot).astype(BF16)
    v_ref[...] = _dot(nkv, wv_ref[...]).astype(BF16)

    naq_ref[...] = (z[:, o3:o3 + na_w] * na_scale).astype(BF16)
    nak_ref[...] = z[:, o3 + na_w:o3 + 2 * na_w].astype(BF16)
    nav_ref[...] = z[:, o3 + 2 * na_w:o3 + 3 * na_w].astype(BF16)


def _attn_in(x, g, w_all, qn, wq, wqs, kvn, wk, wv, cos_t, sin_t, *, tile, seq):
    n, d = x.shape
    q_lora = qn.shape[1]
    kv_lora = kvn.shape[1]
    hw = MLA_HEADS * LANES
    vw = MLA_HEADS * MLA_V
    na_w = NA_HEADS * NA_HD
    per_seq = seq // tile

    tok = lambda w: pl.BlockSpec((tile, w), lambda i: (i, 0))
    rope = pl.BlockSpec((tile, LANES), lambda i: (lax.rem(i, per_seq), 0))
    kern = functools.partial(_attn_in_kernel, q_lora=q_lora, kv_lora=kv_lora,
                             q_scale=float((MLA_NOPE + MLA_ROPE) ** -0.5 * LOG2E),
                             na_scale=float(NA_HD ** -0.5 * LOG2E))
    return pl.pallas_call(
        kern,
        grid=(n // tile,),
        in_specs=[tok(d), _const_spec(g.shape), _const_spec(w_all.shape), _const_spec(qn.shape),
                  _const_spec(wq.shape), _const_spec(wqs.shape), _const_spec(kvn.shape),
                  _const_spec(wk.shape), _const_spec(wv.shape), rope, rope],
        out_specs=[tok(hw), tok(hw), tok(vw), tok(na_w), tok(na_w), tok(na_w)],
        out_shape=[jax.ShapeDtypeStruct((n, hw), BF16), jax.ShapeDtypeStruct((n, hw), BF16),
                   jax.ShapeDtypeStruct((n, vw), BF16),
                   jax.ShapeDtypeStruct((n, na_w), BF16), jax.ShapeDtypeStruct((n, na_w), BF16),
                   jax.ShapeDtypeStruct((n, na_w), BF16)],
        compiler_params=_params(("parallel",)),
        name="attn_in",
    )(x, g, w_all, qn, wq, wqs, kvn, wk, wv, cos_t, sin_t)


def _mla_kernel(q_ref, k_ref, v_ref, o_ref):
    tq = q_ref.shape[0]
    seq = k_ref.shape[0]
    heads = [slice(hd * LANES, (hd + 1) * LANES) for hd in range(2)]
    scores = [_dot_nt(q_ref[:, hsl], k_ref[:, hsl]) for hsl in heads]
    probs = [jnp.exp2((s - jnp.max(s, axis=-1, keepdims=True)).astype(BF16)) for s in scores]
    v_ext = jnp.concatenate([v_ref[...], jnp.ones((seq, LANES), BF16)], axis=1)
    accs = [_dot(p, v_ext) for p in probs]
    outs = [acc[:, :LANES] / acc[:, LANES:LANES + 1] for acc in accs]
    lane = lax.broadcasted_iota(jnp.int32, (tq, LANES), 1)
    o_ref[...] = jnp.where(lane < MLA_V, outs[0], outs[1]).astype(BF16)


def _mla(q, k, v, *, batch, seq, tq):
    n = q.shape[0]
    nq = seq // tq
    return pl.pallas_call(
        _mla_kernel,
        grid=(batch, MLA_HEADS // 2, nq),
        in_specs=[pl.BlockSpec((tq, 2 * LANES), lambda b, hp, i: (b * nq + i, hp)),
                  pl.BlockSpec((seq, 2 * LANES), lambda b, hp, i: (b, hp)),
                  pl.BlockSpec((seq, LANES), lambda b, hp, i: (b, hp))],
        out_specs=pl.BlockSpec((tq, LANES), lambda b, hp, i: (b * nq + i, hp)),
        out_shape=jax.ShapeDtypeStruct((n, MLA_HEADS * MLA_V), BF16),
        compiler_params=_params(("parallel", "parallel", "arbitrary")),
        name="mla_attn",
    )(q, k, v)


NA_GROUP = 4


def _na_kernel(q_ref, k_ref, v_ref, bias_ref, o_ref, *, rows):
    win = NA_KH * GRID_W
    lane_q = lax.broadcasted_iota(jnp.int32, (GRID_W, LANES), 1)
    ones = jnp.ones((win, LANES), BF16)

    def body(g, _):
        q0s, k0s, scores = [], [], []
        for j in range(NA_GROUP):
            r = g * NA_GROUP + j
            rs = jnp.clip(r - NA_KH // 2, 0, rows - NA_KH)
            q0 = pl.multiple_of(r * GRID_W, GRID_W)
            k0 = pl.multiple_of(rs * GRID_W, GRID_W)
            qr = q_ref[pl.ds(q0, GRID_W), :]
            zero = jnp.zeros_like(qr)
            q2 = jnp.concatenate([jnp.where(lane_q < NA_HD, qr, zero), jnp.where(lane_q >= NA_HD, qr, zero)],
                                 axis=0)
            scores.append(_dot_nt(q2, k_ref[pl.ds(k0, win), :]) + bias_ref[r - rs])
            q0s.append(q0)
            k0s.append(k0)
        probs = [jnp.exp2((s - jnp.max(s, axis=-1, keepdims=True)).astype(BF16)) for s in scores]
        accs = [_dot(p, jnp.concatenate([v_ref[pl.ds(k0, win), :], ones], axis=1)) for p, k0 in zip(probs, k0s)]
        for q0, acc in zip(q0s, accs):
            o2 = acc[:, :LANES] / acc[:, LANES:LANES + 1]
            o_ref[pl.ds(q0, GRID_W), :] = jnp.where(lane_q < NA_HD, o2[:GRID_W], o2[GRID_W:]).astype(BF16)
        return 0

    lax.fori_loop(0, rows // NA_GROUP, body, 0)


def _na(q, k, v, bias, *, batch, seq):
    n = q.shape[0]
    rows = seq // GRID_W
    blk = pl.BlockSpec((seq, LANES), lambda b, hp: (b, hp))
    return pl.pallas_call(
        functools.partial(_na_kernel, rows=rows),
        grid=(batch, NA_HEADS // 2),
        in_specs=[blk, blk, blk, pl.BlockSpec((None,) + bias.shape[1:], lambda b, hp: (hp, 0, 0, 0))],
        out_specs=blk,
        out_shape=jax.ShapeDtypeStruct((n, NA_HEADS * NA_HD), BF16),
        compiler_params=_params(("parallel", "parallel")),
        name="na_attn",
    )(q, k, v, bias)


def _na_bias_table(rpb):
    n_rel = 2 * NA_KW - 1
    period = 2 * GRID_W
    by_row = jnp.stack([rpb[:, NA_KH - 1 - dl:2 * NA_KH - 1 - dl, :] for dl in range(NA_KH)], axis=1)
    left = GRID_W - NA_KW
    p = jnp.pad(by_row.astype(F32), ((0, 0), (0, 0), (0, 0), (left, period - n_rel - left)))
    lead = p.shape[:-1]
    flat = jnp.broadcast_to(p[..., None, :], lead + (GRID_W, period)).reshape(lead + (GRID_W * period,))
    skew = flat[..., :GRID_W * (period - 1)].reshape(lead + (GRID_W, period - 1))
    t = skew[..., GRID_W - 1:2 * GRID_W - 1]
    cols = np.arange(GRID_W)
    col_start = np.clip(cols - NA_KW // 2, 0, GRID_W - NA_KW)
    kc = np.arange(GRID_W)
    in_win = (kc[None, :] >= col_start[:, None]) & (kc[None, :] < col_start[:, None] + NA_KW)
    t = jnp.where(jnp.asarray(in_win)[None, None, None], t * LOG2E, NEG)
    t = jnp.swapaxes(t, 2, 3).reshape(rpb.shape[0] // 2, 2, NA_KH, GRID_W, NA_KH * GRID_W)
    return jnp.swapaxes(t, 1, 2).reshape(rpb.shape[0] // 2, NA_KH, 2 * GRID_W, NA_KH * GRID_W)


def _attn_out_kernel(om_ref, on_ref, wm_ref, wn_ref, x_ref, gpost_ref, gnext_ref, xo_ref, ho_ref):
    m = _dot(om_ref[...], wm_ref[...]) + _dot(on_ref[...], wn_ref[...])
    xo = x_ref[...] + _rms(m, gpost_ref[...])
    xo_ref[...] = xo
    ho_ref[...] = _rms(xo, gnext_ref[...]).astype(BF16)


def _attn_out(o_mla, o_na, w_m, w_n, x, g_post, g_next, *, tile):
    n, d = x.shape
    tok = lambda w: pl.BlockSpec((tile, w), lambda i: (i, 0))
    return pl.pallas_call(
        _attn_out_kernel,
        grid=(n // tile,),
        in_specs=[tok(o_mla.shape[1]), tok(o_na.shape[1]), _const_spec(w_m.shape), _const_spec(w_n.shape),
                  tok(d), _const_spec(g_post.shape), _const_spec(g_next.shape)],
        out_specs=[tok(d), tok(d)],
        out_shape=[jax.ShapeDtypeStruct((n, d), F32), jax.ShapeDtypeStruct((n, d), BF16)],
        compiler_params=_params(("parallel",)),
        name="attn_out",
    )(o_mla, o_na, w_m, w_n, x, g_post, g_next)


def _halo_specs(tile, d, n):
    per = tile // HALO
    last = n // HALO - 1
    return [pl.BlockSpec((HALO, d), lambda i, *_: (jnp.maximum(i * per - 1, 0), 0)),
            pl.BlockSpec((tile, d), lambda i, *_: (i, 0)),
            pl.BlockSpec((HALO, d), lambda i, *_: (jnp.minimum((i + 1) * per, last), 0))]


def _fill_halo(hh_ref, hp_ref, hm_ref, hn_ref, pos0, tile, seq):
    zero = jnp.zeros(hp_ref.shape, hp_ref.dtype)
    hh_ref[0:HALO, :] = jnp.where(pos0 == 0, zero, hp_ref[...])
    hh_ref[HALO:HALO + tile, :] = hm_ref[...]
    hh_ref[HALO + tile:, :] = jnp.where(pos0 + tile == seq, zero, hn_ref[...])


def _dwconv(u, w, b, tile, left):
    y = b
    for kk in range(w.shape[0]):
        s = HALO - left + kk
        y = y + w[kk:kk + 1, :] * u[s:s + tile, :]
    return y


def _ffn_kernel(*refs, tile, seq, emit_next):
    if emit_next:
        (hp_ref, hm_ref, hn_ref, x_ref, wg_ref, wv_ref, cwg_ref, cwv_ref, cbg_ref, cbv_ref, wd_ref,
         gpost_ref, gnext_ref, xo_ref, ho_ref, hh_ref, acc_ref) = refs
    else:
        (hp_ref, hm_ref, hn_ref, x_ref, wg_ref, wv_ref, cwg_ref, cwv_ref, cbg_ref, cbv_ref, wd_ref,
         gpost_ref, xo_ref, hh_ref, acc_ref) = refs
    i = pl.program_id(0)
    c = pl.program_id(1)

    @pl.when(c == 0)
    def _():
        _fill_halo(hh_ref, hp_ref, hm_ref, hn_ref, lax.rem(i * tile, seq), tile, seq)
        acc_ref[...] = jnp.zeros(acc_ref.shape, F32)

    hh = hh_ref[...]
    left = (cwg_ref.shape[0] - 1) // 2
    g = _dwconv(_dot(hh, wg_ref[...]), cwg_ref[...], cbg_ref[...], tile, left)
    val = _dwconv(_dot(hh, wv_ref[...]), cwv_ref[...], cbv_ref[...], tile, left)
    a = (_gelu_tanh(g) * val).astype(BF16)
    acc_ref[...] += _dot(a, wd_ref[...])

    @pl.when(c == pl.num_programs(1) - 1)
    def _():
        xo = x_ref[...] + _rms(acc_ref[...], gpost_ref[...])
        xo_ref[...] = xo
        if emit_next:
            ho_ref[...] = _rms(xo, gnext_ref[...]).astype(BF16)


def _ffn(h, x, w_up, conv_w, conv_b, w_down, g_post, g_next, *, tile, fc, seq):
    n, d = x.shape
    d_ff = w_down.shape[0]
    nc = d_ff // fc
    kk = conv_w.shape[0]
    emit_next = g_next is not None
    tok = pl.BlockSpec((tile, d), lambda i, c: (i, 0))
    in_specs = _halo_specs(tile, d, n) + [
        tok,
        pl.BlockSpec((d, fc), lambda i, c: (0, c)),
        pl.BlockSpec((d, fc), lambda i, c: (0, nc + c)),
        pl.BlockSpec((kk, fc), lambda i, c: (0, c)),
        pl.BlockSpec((kk, fc), lambda i, c: (0, nc + c)),
        pl.BlockSpec((1, fc), lambda i, c: (0, c)),
        pl.BlockSpec((1, fc), lambda i, c: (0, nc + c)),
        pl.BlockSpec((fc, d), lambda i, c: (c, 0)),
        _const_spec(g_post.shape)]
    args = [h, h, h, x, w_up, w_up, conv_w, conv_w, conv_b, conv_b, w_down, g_post]
    out_specs = [tok]
    out_shape = [jax.ShapeDtypeStruct((n, d), F32)]
    if emit_next:
        in_specs.append(_const_spec(g_next.shape))
        args.append(g_next)
        out_specs.append(tok)
        out_shape.append(jax.ShapeDtypeStruct((n, d), BF16))
    res = pl.pallas_call(
        functools.partial(_ffn_kernel, tile=tile, seq=seq, emit_next=emit_next),
        grid=(n // tile, nc),
        in_specs=in_specs,
        out_specs=out_specs,
        out_shape=out_shape,
        scratch_shapes=[pltpu.VMEM((tile + 2 * HALO, d), BF16), pltpu.VMEM((tile, d), F32)],
        compiler_params=_params(("parallel", "arbitrary")),
        name="conv_ffn",
    )(*args)
    return (res[0], res[1]) if emit_next else (res[0], None)


def _lru_coeffs(xc, xc_bf, gw_ref, gab_ref, gxb_ref, lam_ref):
    nb, bw, _ = gw_ref.shape
    neg = -lam_ref[...]
    sp = jnp.maximum(neg, 0.0) + jnp.log1p(jnp.exp(-jnp.abs(neg)))
    half_c = (-0.5 * LRU_C) * sp
    half_gab = 0.5 * gab_ref[...]
    half_gxb = 0.5 * gxb_ref[...]
    a_parts, b_parts = [], []
    for blk in range(nb):
        sl = slice(blk * bw, (blk + 1) * bw)
        rg = _dot(xc_bf[:, sl], gw_ref[blk])
        hc = half_c[:, sl]
        log_a = hc * jnp.tanh(0.5 * rg[:, :bw] + half_gab[:, sl]) + hc
        hx = 0.5 * xc[:, sl]
        gated_x = hx * jnp.tanh(0.5 * rg[:, bw:] + half_gxb[:, sl]) + hx
        t = jnp.tanh(log_a)
        one_minus_a2 = (-2.0 * t) / (1.0 - t)
        a_parts.append(jnp.exp2(log_a * LOG2E))
        b_parts.append(jnp.sqrt(one_minus_a2) * gated_x)
    return jnp.concatenate(a_parts, axis=1), jnp.concatenate(b_parts, axis=1)


def _scan_tile(a_ref, b_ref, carry_ref, *, reverse):
    t, ch = a_ref.shape
    n_chunks = t // SUB
    row = lax.broadcasted_iota(jnp.int32, (SUB, ch), 0)

    def body(j, carry):
        jj = (n_chunks - 1 - j) if reverse else j
        r0 = pl.multiple_of(jj * SUB, SUB)
        a = a_ref[pl.ds(r0, SUB), :]
        b = b_ref[pl.ds(r0, SUB), :]
        for dist in (1, 2, 4):
            shift = (SUB - dist) if reverse else dist
            a_sh = pltpu.roll(a, shift, axis=0)
            b_sh = pltpu.roll(b, shift, axis=0)
            valid = (row < SUB - dist) if reverse else (row >= dist)
            b = jnp.where(valid, a * b_sh + b, b)
            a = jnp.where(valid, a * a_sh, a)
        hcur = a * carry + b
        b_ref[pl.ds(r0, SUB), :] = hcur
        edge = hcur[0:1, :] if reverse else hcur[SUB - 1:SUB, :]
        return jnp.broadcast_to(edge, (SUB, ch))

    carry_ref[...] = lax.fori_loop(0, n_chunks, body, carry_ref[...], unroll=2)


def _rec_fwd_kernel(hp_ref, hm_ref, hn_ref, wg_ref, wx_ref, cw_ref, cb_ref, gw_ref, gab_ref, gxb_ref,
                    lam_ref, gate_ref, xc_ref, hf_ref, hh_ref, a_ref, b_ref, carry_ref, *, tile, seq):
    pos0 = lax.rem(pl.program_id(0) * tile, seq)
    _fill_halo(hh_ref, hp_ref, hm_ref, hn_ref, pos0, tile, seq)

    @pl.when(pos0 == 0)
    def _():
        carry_ref[...] = jnp.zeros(carry_ref.shape, F32)

    gate_ref[...] = _dot(hm_ref[...], wg_ref[...]).astype(BF16)
    left = (cw_ref.shape[0] - 1) // 2
    xc = _dwconv(_dot(hh_ref[...], wx_ref[...]), cw_ref[...], cb_ref[...], tile, left)
    xc_bf = xc.astype(BF16)
    xc_ref[...] = xc_bf
    a, b = _lru_coeffs(xc, xc_bf, gw_ref, gab_ref, gxb_ref, lam_ref)
    a_ref[...] = a
    b_ref[...] = b
    _scan_tile(a_ref, b_ref, carry_ref, reverse=False)
    hf_ref[...] = b_ref[...].astype(BF16)


def _rec_fwd(h, w_g, w_x, conv_w, conv_b, gw, gab, gxb, lam, *, tile, seq):
    n, d = h.shape
    d_rnn = w_x.shape[1]
    tok = pl.BlockSpec((tile, d_rnn), lambda i: (i, 0))
    in_specs = _halo_specs(tile, d, n) + [_const_spec(a.shape) for a in
                                          (w_g, w_x, conv_w, conv_b, gw, gab, gxb, lam)]
    out = jax.ShapeDtypeStruct((n, d_rnn), BF16)
    return pl.pallas_call(
        functools.partial(_rec_fwd_kernel, tile=tile, seq=seq),
        grid=(n // tile,),
        in_specs=in_specs,
        out_specs=[tok, tok, tok],
        out_shape=[out, out, out],
        scratch_shapes=[pltpu.VMEM((tile + 2 * HALO, d), BF16), pltpu.VMEM((tile, d_rnn), F32),
                        pltpu.VMEM((tile, d_rnn), F32), pltpu.VMEM((SUB, d_rnn), F32)],
        compiler_params=_params(("arbitrary",)),
        name="rglru_fwd",
    )(h, h, h, w_g, w_x, conv_w, conv_b, gw, gab, gxb, lam)


def _rec_bwd_kernel(xc_ref, gate_ref, hf_ref, x_ref, gw_ref, gab_ref, gxb_ref, lam_ref, wo_ref,
                    gpost_ref, gnext_ref, xo_ref, ho_ref, a_ref, b_ref, carry_ref, *, tile, seq):
    i = pl.num_programs(0) - 1 - pl.program_id(0)
    pos0 = lax.rem(i * tile, seq)

    @pl.when(pos0 + tile == seq)
    def _():
        carry_ref[...] = jnp.zeros(carry_ref.shape, F32)

    xc_bf = xc_ref[...]
    a, b = _lru_coeffs(xc_bf.astype(F32), xc_bf, gw_ref, gab_ref, gxb_ref, lam_ref)
    a_ref[...] = a
    b_ref[...] = b
    _scan_tile(a_ref, b_ref, carry_ref, reverse=True)
    y = (hf_ref[...].astype(F32) + b_ref[...]) * _gelu_tanh(gate_ref[...].astype(F32))
    m = _dot(y.astype(BF16), wo_ref[...])
    xo = x_ref[...] + _rms(m, gpost_ref[...])
    xo_ref[...] = xo
    ho_ref[...] = _rms(xo, gnext_ref[...]).astype(BF16)


def _rec_bwd(xc, gate, hf, x, gw, gab, gxb, lam, w_out, g_post, g_next, *, tile, seq):
    n, d = x.shape
    d_rnn = xc.shape[1]
    nt = n // tile
    rtok = lambda w: pl.BlockSpec((tile, w), lambda i: (nt - 1 - i, 0))
    in_specs = [rtok(d_rnn), rtok(d_rnn), rtok(d_rnn), rtok(d)] + [
        _const_spec(a.shape) for a in (gw, gab, gxb, lam, w_out, g_post, g_next)]
    return pl.pallas_call(
        functools.partial(_rec_bwd_kernel, tile=tile, seq=seq),
        grid=(nt,),
        in_specs=in_specs,
        out_specs=[rtok(d), rtok(d)],
        out_shape=[jax.ShapeDtypeStruct((n, d), F32), jax.ShapeDtypeStruct((n, d), BF16)],
        scratch_shapes=[pltpu.VMEM((tile, d_rnn), F32), pltpu.VMEM((tile, d_rnn), F32),
                        pltpu.VMEM((SUB, d_rnn), F32)],
        compiler_params=_params(("arbitrary",)),
        name="rglru_bwd",
    )(xc, gate, hf, x, gw, gab, gxb, lam, w_out, g_post, g_next)


def _prenorm_kernel(x_ref, g_ref, h_ref):
    h_ref[...] = _rms(x_ref[...], g_ref[...]).astype(BF16)


def _prenorm(x, g, *, tile):
    n, d = x.shape
    tok = pl.BlockSpec((tile, d), lambda i: (i, 0))
    return pl.pallas_call(
        _prenorm_kernel, grid=(n // tile,), in_specs=[tok, _const_spec(g.shape)], out_specs=tok,
        out_shape=jax.ShapeDtypeStruct((n, d), BF16), compiler_params=_params(("parallel",)),
        name="prenorm",
    )(x, g)


def _row(v):
    return v.reshape(1, -1).astype(F32)


def _rope_tables(s_max):
    pos = jnp.arange(s_max, dtype=F32)
    inv = ROPE_BASE ** (-jnp.arange(0, MLA_ROPE, 2, dtype=F32) / MLA_ROPE)
    ang = pos[:, None] * inv[None, :]
    cos, sin = jnp.cos(ang), jnp.sin(ang)
    pad = LANES - MLA_NOPE - MLA_ROPE
    cos_t = jnp.concatenate([jnp.ones((s_max, MLA_NOPE), F32), cos, cos, jnp.zeros((s_max, pad), F32)], axis=1)
    sin_t = jnp.concatenate([jnp.zeros((s_max, MLA_NOPE), F32), -sin, sin, jnp.zeros((s_max, pad), F32)], axis=1)
    return cos_t, sin_t


def _swap_halves(w):
    half = w.shape[-1] // 2
    return jnp.concatenate([w[..., half:], w[..., :half]], axis=-1)


def _attn_weights(w_in, w_uq, w_ukv):
    d = w_in.shape[0]
    q_lora = w_uq.shape[0]
    kv_lora = w_ukv.shape[0]
    o1 = q_lora + kv_lora
    o2 = o1 + MLA_ROPE
    pad = LANES - MLA_NOPE - MLA_ROPE
    kpe = w_in[:, o1:o2]
    zl = jnp.zeros((d, MLA_NOPE), w_in.dtype)
    zr = jnp.zeros((d, pad), w_in.dtype)
    w_all = jnp.concatenate([w_in[:, :o1], zl, kpe, zr, zl, _swap_halves(kpe), zr, w_in[:, o2:]], axis=1)

    wq = w_uq.reshape(q_lora, MLA_HEADS, MLA_NOPE + MLA_ROPE)
    zq = jnp.zeros((q_lora, MLA_HEADS, pad), w_uq.dtype)
    wq_p = jnp.concatenate([wq, zq], axis=-1).reshape(q_lora, MLA_HEADS * LANES)
    wq_s = jnp.concatenate([jnp.zeros_like(wq[..., :MLA_NOPE]), _swap_halves(wq[..., MLA_NOPE:]), zq],
                           axis=-1).reshape(q_lora, MLA_HEADS * LANES)

    wkv = w_ukv.reshape(kv_lora, MLA_HEADS, MLA_NOPE + MLA_V)
    zk = jnp.zeros((kv_lora, MLA_HEADS, LANES - MLA_NOPE), w_ukv.dtype)
    wk_p = jnp.concatenate([wkv[..., :MLA_NOPE], zk], axis=-1).reshape(kv_lora, MLA_HEADS * LANES)
    wv = wkv[..., MLA_NOPE:].reshape(kv_lora, MLA_HEADS * MLA_V)
    return tuple(a.astype(BF16) for a in (w_all, wq_p, wq_s, wk_p, wv))


TOKEN_TILE = 512
FFN_TILE = 1024
FFN_CHUNK = 1024
REC_TILE = 512
MLA_TQ = 256


def kernel(x_prompt, x_sample, norm_mix_pre, norm_mix_post, norm_ffn_pre, norm_ffn_post, w_in_attn, q_norm, w_uq, kv_norm, w_ukv, na_rpb, w_out_attn, w_in_rec, conv_w_rec, conv_b_rec, gate_a_w, gate_a_b, gate_x_w, gate_x_b, lru_lambda, w_out_rec, w_ffn_up, conv_w_ffn, conv_b_ffn, w_ffn_down):
    depth = norm_mix_pre.shape[0]
    d_rnn = w_out_rec.shape[1]
    n_mla = MLA_HEADS * MLA_V
    cos_t, sin_t = _rope_tables(max(x_prompt.shape[1], x_sample.shape[1]))

    layers = []
    for li in range(depth):
        j = li // 2
        lw = dict(ffn_up=w_ffn_up[li].astype(BF16), ffn_down=w_ffn_down[li].astype(BF16))
        if li % 2 == 0:
            lw["attn"] = _attn_weights(w_in_attn[j], w_uq[j], w_ukv[j])
            lw["bias"] = _na_bias_table(na_rpb[j])
            w_o = w_out_attn[j].astype(BF16)
            lw["out"] = (w_o[:n_mla], w_o[n_mla:])
        else:
            w_in = w_in_rec[j].astype(BF16)
            lw["rec_in"] = (w_in[:, :d_rnn], w_in[:, d_rnn:])
            lw["gates"] = [jnp.concatenate([gate_a_w[j, e], gate_x_w[j, e]], axis=-1).astype(BF16)
                           for e in range(2)]
            lw["rec_out"] = w_out_rec[j].astype(BF16)
        layers.append(lw)

    def trunk(x3):
        batch, seq, d = x3.shape
        x = x3.reshape(batch * seq, d)
        tile = min(TOKEN_TILE, seq)
        h = None
        for li, lw in enumerate(layers):
            j = li // 2
            if li % 2 == 0:
                w_all, wq_p, wq_s, wk_p, wv = lw["attn"]
                q, k, v, naq, nak, nav = _attn_in(
                    x, _row(norm_mix_pre[li]), w_all, _row(q_norm[j]), wq_p, wq_s, _row(kv_norm[j]), wk_p, wv,
                    cos_t, sin_t, tile=tile, seq=seq)
                o_mla = _mla(q, k, v, batch=batch, seq=seq, tq=min(MLA_TQ, seq))
                o_na = _na(naq, nak, nav, lw["bias"], batch=batch, seq=seq)
                x, h = _attn_out(o_mla, o_na, lw["out"][0], lw["out"][1], x, _row(norm_mix_post[li]),
                                 _row(norm_ffn_pre[li]), tile=tile)
            else:
                if h is None:
                    h = _prenorm(x, _row(norm_mix_pre[li]), tile=tile)
                rec_tile = min(REC_TILE, seq)
                gate, xc, hf = _rec_fwd(h, lw["rec_in"][0], lw["rec_in"][1], conv_w_rec[j], _row(conv_b_rec[j]),
                                        lw["gates"][0], _row(gate_a_b[j, 0]), _row(gate_x_b[j, 0]),
                                        _row(lru_lambda[j, 0]), tile=rec_tile, seq=seq)
                x, h = _rec_bwd(xc, gate, hf, x, lw["gates"][1], _row(gate_a_b[j, 1]), _row(gate_x_b[j, 1]),
                                _row(lru_lambda[j, 1]), lw["rec_out"], _row(norm_mix_post[li]),
                                _row(norm_ffn_pre[li]), tile=rec_tile, seq=seq)
            g_next = _row(norm_mix_pre[li + 1]) if li + 1 < depth else None
            x, h = _ffn(h, x, lw["ffn_up"], conv_w_ffn[li], _row(conv_b_ffn[li]), lw["ffn_down"],
                        _row(norm_ffn_post[li]), g_next, tile=min(FFN_TILE, seq),
                        fc=min(FFN_CHUNK, lw["ffn_down"].shape[0]), seq=seq)
        return x.reshape(batch, seq, d)

    return trunk(x_prompt), trunk(x_sample)
```

```python
import functools
import math

import numpy as np
import jax
import jax.numpy as jnp
from jax import lax
from jax.experimental import pallas as pl
from jax.experimental.pallas import tpu as pltpu

F32 = jnp.float32
BF16 = jnp.bfloat16

EPS = 1e-6
GRID_W = 64
MLA_HEADS = 8
MLA_NOPE = 64
MLA_ROPE = 32
MLA_V = 64
ROPE_BASE = 10000.0
NA_HEADS = 8
NA_HD = 64
NA_KH = 8
NA_KW = 16
LRU_C = 8.0

LANES = 128
SUB = 8
HALO = 16
NEG = -1e30
VMEM_LIMIT = 56 * 1024 * 1024
LOG2E = math.log2(math.e)


def _rms(x, g):
    return x * lax.rsqrt(jnp.mean(x * x, axis=-1, keepdims=True) + EPS) * g


def _gelu_tanh(x):
    c1 = math.sqrt(2.0 / math.pi)
    k1 = -2.0 * c1 * LOG2E
    k2 = k1 * 0.044715
    e = jnp.exp2(x * ((x * x) * k2 + k1))
    return x / (1.0 + e)


def _dot(a, b):
    return jnp.dot(a, b, preferred_element_type=F32)


def _dot_nt(a, b):
    return lax.dot_general(a, b, (((1,), (1,)), ((), ())), preferred_element_type=F32)


def _params(sem):
    return pltpu.CompilerParams(dimension_semantics=sem, vmem_limit_bytes=VMEM_LIMIT)


def _const_spec(shape):
    nd = len(shape)
    return pl.BlockSpec(shape, lambda *_: (0,) * nd)


def _attn_in_kernel(x_ref, g_ref, win_ref, qn_ref, wq_ref, wqs_ref, kvn_ref, wk_ref, wv_ref,
                    cos_ref, sin_ref,
                    q_ref, k_ref, v_ref, naq_ref, nak_ref, nav_ref, *, q_lora, kv_lora, q_scale, na_scale):
    h = _rms(x_ref[...], g_ref[...]).astype(BF16)
    z = _dot(h, win_ref[...])
    o0 = q_lora
    o1 = o0 + kv_lora
    o2 = o1 + LANES
    o3 = o2 + LANES
    na_w = NA_HEADS * NA_HD
    cos = cos_ref[...]
    sin = sin_ref[...]

    nq = _rms(z[:, :o0], qn_ref[...]).astype(BF16)
    q = _dot(nq, wq_ref[...])
    qs = _dot(nq, wqs_ref[...])
    for hd in range(MLA_HEADS):
        sl = slice(hd * LANES, (hd + 1) * LANES)
        q_ref[:, sl] = ((q[:, sl] * cos + qs[:, sl] * sin) * q_scale).astype(BF16)

    nkv = _rms(z[:, o0:o1], kvn_ref[...]).astype(BF16)
    k = _dot(nkv, wk_ref[...])
    rot = z[:, o1:o2] * cos + z[:, o2:o3] * sin
    for hd in range(MLA_HEADS):
        sl = slice(hd * LANES, (hd + 1) * LANES)
        k_ref[:, sl] = (k[:, sl] + rot).astype(BF16)
    v_ref[...] = _dot(nkv, wv_ref[...]).astype(BF16)

    naq_ref[...] = (z[:, o3:o3 + na_w] * na_scale).astype(BF16)
    nak_ref[...] = z[:, o3 + na_w:o3 + 2 * na_w].astype(BF16)
    nav_ref[...] = z[:, o3 + 2 * na_w:o3 + 3 * na_w].astype(BF16)


def _attn_in(x, g, w_all, qn, wq, wqs, kvn, wk, wv, cos_t, sin_t, *, tile, seq):
    n, d = x.shape
    q_lora = qn.shape[1]
    kv_lora = kvn.shape[1]
    hw = MLA_HEADS * LANES
    vw = MLA_HEADS * MLA_V
    na_w = NA_HEADS * NA_HD
    per_seq = seq // tile

    tok = lambda w: pl.BlockSpec((tile, w), lambda i: (i, 0))
    rope = pl.BlockSpec((tile, LANES), lambda i: (lax.rem(i, per_seq), 0))
    kern = functools.partial(_attn_in_kernel, q_lora=q_lora, kv_lora=kv_lora,
                             q_scale=float((MLA_NOPE + MLA_ROPE) ** -0.5 * LOG2E),
                             na_scale=float(NA_HD ** -0.5 * LOG2E))
    return pl.pallas_call(
        kern,
        grid=(n // tile,),
        in_specs=[tok(d), _const_spec(g.shape), _const_spec(w_all.shape), _const_spec(qn.shape),
                  _const_spec(wq.shape), _const_spec(wqs.shape), _const_spec(kvn.shape),
                  _const_spec(wk.shape), _const_spec(wv.shape), rope, rope],
        out_specs=[tok(hw), tok(hw), tok(vw), tok(na_w), tok(na_w), tok(na_w)],
        out_shape=[jax.ShapeDtypeStruct((n, hw), BF16), jax.ShapeDtypeStruct((n, hw), BF16),
                   jax.ShapeDtypeStruct((n, vw), BF16),
                   jax.ShapeDtypeStruct((n, na_w), BF16), jax.ShapeDtypeStruct((n, na_w), BF16),
                   jax.ShapeDtypeStruct((n, na_w), BF16)],
        compiler_params=_params(("parallel",)),
        name="attn_in",
    )(x, g, w_all, qn, wq, wqs, kvn, wk, wv, cos_t, sin_t)


def _mla_kernel(q_ref, k_ref, v_ref, o_ref):
    tq = q_ref.shape[0]
    seq = k_ref.shape[0]
    heads = [slice(hd * LANES, (hd + 1) * LANES) for hd in range(2)]
    scores = [_dot_nt(q_ref[:, hsl], k_ref[:, hsl]) for hsl in heads]
    probs = [jnp.exp2((s - jnp.max(s, axis=-1, keepdims=True)).astype(BF16)) for s in scores]
    v_ext = jnp.concatenate([v_ref[...], jnp.ones((seq, LANES), BF16)], axis=1)
    accs = [_dot(p, v_ext) for p in probs]
    outs = [acc[:, :LANES] / acc[:, LANES:LANES + 1] for acc in accs]
    lane = lax.broadcasted_iota(jnp.int32, (tq, LANES), 1)
    o_ref[...] = jnp.where(lane < MLA_V, outs[0], outs[1]).astype(BF16)


def _mla(q, k, v, *, batch, seq, tq):
    n = q.shape[0]
    nq = seq // tq
    return pl.pallas_call(
        _mla_kernel,
        grid=(batch, MLA_HEADS // 2, nq),
        in_specs=[pl.BlockSpec((tq, 2 * LANES), lambda b, hp, i: (b * nq + i, hp)),
                  pl.BlockSpec((seq, 2 * LANES), lambda b, hp, i: (b, hp)),
                  pl.BlockSpec((seq, LANES), lambda b, hp, i: (b, hp))],
        out_specs=pl.BlockSpec((tq, LANES), lambda b, hp, i: (b * nq + i, hp)),
        out_shape=jax.ShapeDtypeStruct((n, MLA_HEADS * MLA_V), BF16),
        compiler_params=_params(("parallel", "parallel", "arbitrary")),
        name="mla_attn",
    )(q, k, v)


NA_GROUP = 16


def _na_kernel(q_ref, k_ref, v_ref, bias_ref, o_ref, *, rows):
    win = NA_KH * GRID_W
    lane_q = lax.broadcasted_iota(jnp.int32, (GRID_W, LANES), 1)
    ones = jnp.ones((win, LANES), BF16)
    group = math.gcd(NA_GROUP, rows)

    def body(g, _):
        q0s, k0s, scores = [], [], []
        for j in range(group):
            r = g * group + j
            rs = jnp.clip(r - NA_KH // 2, 0, rows - NA_KH)
            q0 = pl.multiple_of(r * GRID_W, GRID_W)
            k0 = pl.multiple_of(rs * GRID_W, GRID_W)
            qr = q_ref[pl.ds(q0, GRID_W), :]
            zero = jnp.zeros_like(qr)
            q2 = jnp.concatenate([jnp.where(lane_q < NA_HD, qr, zero), jnp.where(lane_q >= NA_HD, qr, zero)],
                                 axis=0)
            scores.append(_dot_nt(q2, k_ref[pl.ds(k0, win), :]) + bias_ref[r - rs])
            q0s.append(q0)
            k0s.append(k0)
        probs = [jnp.exp2((s - jnp.max(s, axis=-1, keepdims=True)).astype(BF16)) for s in scores]
        accs = [_dot(p, jnp.concatenate([v_ref[pl.ds(k0, win), :], ones], axis=1)) for p, k0 in zip(probs, k0s)]
        for q0, acc in zip(q0s, accs):
            o2 = acc[:, :LANES] / acc[:, LANES:LANES + 1]
            o_ref[pl.ds(q0, GRID_W), :] = jnp.where(lane_q < NA_HD, o2[:GRID_W], o2[GRID_W:]).astype(BF16)
        return 0

    lax.fori_loop(0, rows // group, body, 0)


def _na(q, k, v, bias, *, batch, seq):
    n = q.shape[0]
    rows = seq // GRID_W
    blk = pl.BlockSpec((seq, LANES), lambda b, hp: (b, hp))
    return pl.pallas_call(
        functools.partial(_na_kernel, rows=rows),
        grid=(batch, NA_HEADS // 2),
        in_specs=[blk, blk, blk, pl.BlockSpec((None,) + bias.shape[1:], lambda b, hp: (hp, 0, 0, 0))],
        out_specs=blk,
        out_shape=jax.ShapeDtypeStruct((n, NA_HEADS * NA_HD), BF16),
        compiler_params=_params(("parallel", "parallel")),
        name="na_attn",
    )(q, k, v, bias)


def _na_bias_table(rpb):
    n_rel = 2 * NA_KW - 1
    period = 2 * GRID_W
    by_row = jnp.stack([rpb[:, NA_KH - 1 - dl:2 * NA_KH - 1 - dl, :] for dl in range(NA_KH)], axis=1)
    left = GRID_W - NA_KW
    p = jnp.pad(by_row.astype(F32), ((0, 0), (0, 0), (0, 0), (left, period - n_rel - left)))
    lead = p.shape[:-1]
    flat = jnp.broadcast_to(p[..., None, :], lead + (GRID_W, period)).reshape(lead + (GRID_W * period,))
    skew = flat[..., :GRID_W * (period - 1)].reshape(lead + (GRID_W, period - 1))
    t = skew[..., GRID_W - 1:2 * GRID_W - 1]
    cols = np.arange(GRID_W)
    col_start = np.clip(cols - NA_KW // 2, 0, GRID_W - NA_KW)
    kc = np.arange(GRID_W)
    in_win = (kc[None, :] >= col_start[:, None]) & (kc[None, :] < col_start[:, None] + NA_KW)
    t = jnp.where(jnp.asarray(in_win)[None, None, None], t * LOG2E, NEG)
    t = jnp.swapaxes(t, 2, 3).reshape(rpb.shape[0] // 2, 2, NA_KH, GRID_W, NA_KH * GRID_W)
    return jnp.swapaxes(t, 1, 2).reshape(rpb.shape[0] // 2, NA_KH, 2 * GRID_W, NA_KH * GRID_W)


def _attn_out_kernel(om_ref, on_ref, wm_ref, wn_ref, x_ref, gpost_ref, gnext_ref, xo_ref, ho_ref):
    m = _dot(om_ref[...], wm_ref[...]) + _dot(on_ref[...], wn_ref[...])
    xo = x_ref[...] + _rms(m, gpost_ref[...])
    xo_ref[...] = xo
    ho_ref[...] = _rms(xo, gnext_ref[...]).astype(BF16)


def _attn_out(o_mla, o_na, w_m, w_n, x, g_post, g_next, *, tile):
    n, d = x.shape
    tok = lambda w: pl.BlockSpec((tile, w), lambda i: (i, 0))
    return pl.pallas_call(
        _attn_out_kernel,
        grid=(n // tile,),
        in_specs=[tok(o_mla.shape[1]), tok(o_na.shape[1]), _const_spec(w_m.shape), _const_spec(w_n.shape),
                  tok(d), _const_spec(g_post.shape), _const_spec(g_next.shape)],
        out_specs=[tok(d), tok(d)],
        out_shape=[jax.ShapeDtypeStruct((n, d), F32), jax.ShapeDtypeStruct((n, d), BF16)],
        compiler_params=_params(("parallel",)),
        name="attn_out",
    )(o_mla, o_na, w_m, w_n, x, g_post, g_next)


def _halo_specs(tile, d, n):
    per = tile // HALO
    last = n // HALO - 1
    return [pl.BlockSpec((HALO, d), lambda i, *_: (jnp.maximum(i * per - 1, 0), 0)),
            pl.BlockSpec((tile, d), lambda i, *_: (i, 0)),
            pl.BlockSpec((HALO, d), lambda i, *_: (jnp.minimum((i + 1) * per, last), 0))]


def _fill_halo(hh_ref, hp_ref, hm_ref, hn_ref, pos0, tile, seq):
    zero = jnp.zeros(hp_ref.shape, hp_ref.dtype)
    hh_ref[0:HALO, :] = jnp.where(pos0 == 0, zero, hp_ref[...])
    hh_ref[HALO:HALO + tile, :] = hm_ref[...]
    hh_ref[HALO + tile:, :] = jnp.where(pos0 + tile == seq, zero, hn_ref[...])


def _dwconv(u, w, b, tile, left, chained_rolls=False):
    taps = w.shape[0]
    if chained_rolls:
        n = u.shape[0]
        shifted = {0: u}
        for d in range(1, taps - left):
            shifted[d] = pltpu.roll(shifted[d - 1], n - 1, axis=0)
        for d in range(1, left + 1):
            shifted[-d] = pltpu.roll(shifted[1 - d], 1, axis=0)
        y = b
        for kk in range(taps):
            y = y + w[kk:kk + 1, :] * shifted[kk - left][HALO:HALO + tile, :]
        return y
    y = b
    for kk in range(taps):
        s = HALO - left + kk
        y = y + w[kk:kk + 1, :] * u[s:s + tile, :]
    return y


def _ffn_kernel(*refs, tile, seq, emit_next):
    if emit_next:
        (hp_ref, hm_ref, hn_ref, x_ref, wg_ref, wv_ref, cwg_ref, cwv_ref, cbg_ref, cbv_ref, wd_ref,
         gpost_ref, gnext_ref, xo_ref, ho_ref, hh_ref, acc_ref) = refs
    else:
        (hp_ref, hm_ref, hn_ref, x_ref, wg_ref, wv_ref, cwg_ref, cwv_ref, cbg_ref, cbv_ref, wd_ref,
         gpost_ref, xo_ref, hh_ref, acc_ref) = refs
    i = pl.program_id(0)
    c = pl.program_id(1)

    @pl.when(c == 0)
    def _():
        _fill_halo(hh_ref, hp_ref, hm_ref, hn_ref, lax.rem(i * tile, seq), tile, seq)
        acc_ref[...] = jnp.zeros(acc_ref.shape, F32)

    hh = hh_ref[...]
    left = (cwg_ref.shape[0] - 1) // 2
    g = _dwconv(_dot(hh, wg_ref[...]), cwg_ref[...], cbg_ref[...], tile, left)
    val = _dwconv(_dot(hh, wv_ref[...]), cwv_ref[...], cbv_ref[...], tile, left)
    a = (_gelu_tanh(g) * val).astype(BF16)
    acc_ref[...] += _dot(a, wd_ref[...])

    @pl.when(c == pl.num_programs(1) - 1)
    def _():
        xo = x_ref[...] + _rms(acc_ref[...], gpost_ref[...])
        xo_ref[...] = xo
        if emit_next:
            ho_ref[...] = _rms(xo, gnext_ref[...]).astype(BF16)


def _ffn(h, x, w_up, conv_w, conv_b, w_down, g_post, g_next, *, tile, fc, seq):
    n, d = x.shape
    d_ff = w_down.shape[0]
    nc = d_ff // fc
    kk = conv_w.shape[0]
    emit_next = g_next is not None
    tok = pl.BlockSpec((tile, d), lambda i, c: (i, 0))
    in_specs = _halo_specs(tile, d, n) + [
        tok,
        pl.BlockSpec((d, fc), lambda i, c: (0, c)),
        pl.BlockSpec((d, fc), lambda i, c: (0, nc + c)),
        pl.BlockSpec((kk, fc), lambda i, c: (0, c)),
        pl.BlockSpec((kk, fc), lambda i, c: (0, nc + c)),
        pl.BlockSpec((1, fc), lambda i, c: (0, c)),
        pl.BlockSpec((1, fc), lambda i, c: (0, nc + c)),
        pl.BlockSpec((fc, d), lambda i, c: (c, 0)),
        _const_spec(g_post.shape)]
    args = [h, h, h, x, w_up, w_up, conv_w, conv_w, conv_b, conv_b, w_down, g_post]
    out_specs = [tok]
    out_shape = [jax.ShapeDtypeStruct((n, d), F32)]
    if emit_next:
        in_specs.append(_const_spec(g_next.shape))
        args.append(g_next)
        out_specs.append(tok)
        out_shape.append(jax.ShapeDtypeStruct((n, d), BF16))
    res = pl.pallas_call(
        functools.partial(_ffn_kernel, tile=tile, seq=seq, emit_next=emit_next),
        grid=(n // tile, nc),
        in_specs=in_specs,
        out_specs=out_specs,
        out_shape=out_shape,
        scratch_shapes=[pltpu.VMEM((tile + 2 * HALO, d), BF16), pltpu.VMEM((tile, d), F32)],
        compiler_params=_params(("parallel", "arbitrary")),
        name="conv_ffn",
    )(*args)
    return (res[0], res[1]) if emit_next else (res[0], None)


def _lru_coeffs(xc, xc_bf, gw_ref, gab_ref, gxb_ref, lam_ref):
    nb, bw, _ = gw_ref.shape
    neg = -lam_ref[...]
    sp = jnp.maximum(neg, 0.0) + jnp.log1p(jnp.exp(-jnp.abs(neg)))
    half_c = (-0.5 * LRU_C) * sp
    half_gab = 0.5 * gab_ref[...]
    half_gxb = 0.5 * gxb_ref[...]
    a_parts, b_parts = [], []
    for blk in range(nb):
        sl = slice(blk * bw, (blk + 1) * bw)
        rg = _dot(xc_bf[:, sl], gw_ref[blk])
        hc = half_c[:, sl]
        log_a = hc * jnp.tanh(rg[:, :bw] + half_gab[:, sl]) + hc
        hx = 0.5 * xc[:, sl]
        gated_x = hx * jnp.tanh(rg[:, bw:] + half_gxb[:, sl]) + hx
        t = jnp.tanh(log_a)
        one_minus_a2 = (-2.0 * t) / (1.0 - t)
        root = jnp.where(one_minus_a2 > 0.0, one_minus_a2 * lax.rsqrt(one_minus_a2), 0.0)
        a_parts.append(jnp.exp2(log_a * LOG2E))
        b_parts.append(root * gated_x)
    return jnp.concatenate(a_parts, axis=1), jnp.concatenate(b_parts, axis=1)


def _scan_tile(a_ref, b_ref, carry_ref, *, reverse):
    t, ch = a_ref.shape
    n_chunks = t // SUB
    row = lax.broadcasted_iota(jnp.int32, (SUB, ch), 0)

    def body(j, carry):
        jj = (n_chunks - 1 - j) if reverse else j
        r0 = pl.multiple_of(jj * SUB, SUB)
        a = a_ref[pl.ds(r0, SUB), :]
        b = b_ref[pl.ds(r0, SUB), :]
        for dist in (1, 2, 4):
            shift = (SUB - dist) if reverse else dist
            a_sh = pltpu.roll(a, shift, axis=0)
            b_sh = pltpu.roll(b, shift, axis=0)
            valid = (row < SUB - dist) if reverse else (row >= dist)
            b = jnp.where(valid, a * b_sh + b, b)
            a = jnp.where(valid, a * a_sh, a)
        hcur = a * carry + b
        b_ref[pl.ds(r0, SUB), :] = hcur
        edge = hcur[0:1, :] if reverse else hcur[SUB - 1:SUB, :]
        return jnp.broadcast_to(edge, (SUB, ch))

    carry_ref[...] = lax.fori_loop(0, n_chunks, body, carry_ref[...], unroll=2)


def _rec_fwd_kernel(hp_ref, hm_ref, hn_ref, wg_ref, wx_ref, cw_ref, cb_ref, gw_ref, gab_ref, gxb_ref,
                    lam_ref, gate_ref, xc_ref, hf_ref, hh_ref, a_ref, b_ref, carry_ref, *, tile, seq):
    pos0 = lax.rem(pl.program_id(0) * tile, seq)
    _fill_halo(hh_ref, hp_ref, hm_ref, hn_ref, pos0, tile, seq)

    @pl.when(pos0 == 0)
    def _():
        carry_ref[...] = jnp.zeros(carry_ref.shape, F32)

    gate_ref[...] = _dot(hm_ref[...], wg_ref[...]).astype(BF16)
    left = (cw_ref.shape[0] - 1) // 2
    xc = _dwconv(_dot(hh_ref[...], wx_ref[...]), cw_ref[...], cb_ref[...], tile, left, chained_rolls=True)
    xc_bf = xc.astype(BF16)
    xc_ref[...] = xc_bf
    a, b = _lru_coeffs(xc, xc_bf, gw_ref, gab_ref, gxb_ref, lam_ref)
    a_ref[...] = a
    b_ref[...] = b
    _scan_tile(a_ref, b_ref, carry_ref, reverse=False)
    hf_ref[...] = b_ref[...].astype(BF16)


def _rec_fwd(h, w_g, w_x, conv_w, conv_b, gw, gab, gxb, lam, *, tile, seq):
    n, d = h.shape
    d_rnn = w_x.shape[1]
    tok = pl.BlockSpec((tile, d_rnn), lambda i: (i, 0))
    in_specs = _halo_specs(tile, d, n) + [_const_spec(a.shape) for a in
                                          (w_g, w_x, conv_w, conv_b, gw, gab, gxb, lam)]
    out = jax.ShapeDtypeStruct((n, d_rnn), BF16)
    return pl.pallas_call(
        functools.partial(_rec_fwd_kernel, tile=tile, seq=seq),
        grid=(n // tile,),
        in_specs=in_specs,
        out_specs=[tok, tok, tok],
        out_shape=[out, out, out],
        scratch_shapes=[pltpu.VMEM((tile + 2 * HALO, d), BF16), pltpu.VMEM((tile, d_rnn), F32),
                        pltpu.VMEM((tile, d_rnn), F32), pltpu.VMEM((SUB, d_rnn), F32)],
        compiler_params=_params(("arbitrary",)),
        name="rglru_fwd",
    )(h, h, h, w_g, w_x, conv_w, conv_b, gw, gab, gxb, lam)


def _rec_bwd_kernel(xc_ref, gate_ref, hf_ref, x_ref, gw_ref, gab_ref, gxb_ref, lam_ref, wo_ref,
                    gpost_ref, gnext_ref, xo_ref, ho_ref, a_ref, b_ref, carry_ref, *, tile, seq):
    i = pl.num_programs(0) - 1 - pl.program_id(0)
    pos0 = lax.rem(i * tile, seq)

    @pl.when(pos0 + tile == seq)
    def _():
        carry_ref[...] = jnp.zeros(carry_ref.shape, F32)

    xc_bf = xc_ref[...]
    a, b = _lru_coeffs(xc_bf.astype(F32), xc_bf, gw_ref, gab_ref, gxb_ref, lam_ref)
    a_ref[...] = a
    b_ref[...] = b
    _scan_tile(a_ref, b_ref, carry_ref, reverse=True)
    y = (hf_ref[...].astype(F32) + b_ref[...]) * _gelu_tanh(gate_ref[...].astype(F32))
    m = _dot(y.astype(BF16), wo_ref[...])
    xo = x_ref[...] + _rms(m, gpost_ref[...])
    xo_ref[...] = xo
    ho_ref[...] = _rms(xo, gnext_ref[...]).astype(BF16)


def _rec_bwd(xc, gate, hf, x, gw, gab, gxb, lam, w_out, g_post, g_next, *, tile, seq):
    n, d = x.shape
    d_rnn = xc.shape[1]
    nt = n // tile
    rtok = lambda w: pl.BlockSpec((tile, w), lambda i: (nt - 1 - i, 0))
    in_specs = [rtok(d_rnn), rtok(d_rnn), rtok(d_rnn), rtok(d)] + [
        _const_spec(a.shape) for a in (gw, gab, gxb, lam, w_out, g_post, g_next)]
    return pl.pallas_call(
        functools.partial(_rec_bwd_kernel, tile=tile, seq=seq),
        grid=(nt,),
        in_specs=in_specs,
        out_specs=[rtok(d), rtok(d)],
        out_shape=[jax.ShapeDtypeStruct((n, d), F32), jax.ShapeDtypeStruct((n, d), BF16)],
        scratch_shapes=[pltpu.VMEM((tile, d_rnn), F32), pltpu.VMEM((tile, d_rnn), F32),
                        pltpu.VMEM((SUB, d_rnn), F32)],
        compiler_params=_params(("arbitrary",)),
        name="rglru_bwd",
    )(xc, gate, hf, x, gw, gab, gxb, lam, w_out, g_post, g_next)


def _prenorm_kernel(x_ref, g_ref, h_ref):
    h_ref[...] = _rms(x_ref[...], g_ref[...]).astype(BF16)


def _prenorm(x, g, *, tile):
    n, d = x.shape
    tok = pl.BlockSpec((tile, d), lambda i: (i, 0))
    return pl.pallas_call(
        _prenorm_kernel, grid=(n // tile,), in_specs=[tok, _const_spec(g.shape)], out_specs=tok,
        out_shape=jax.ShapeDtypeStruct((n, d), BF16), compiler_params=_params(("parallel",)),
        name="prenorm",
    )(x, g)


def _row(v):
    return v.reshape(1, -1).astype(F32)


def _rope_tables(s_max):
    pos = jnp.arange(s_max, dtype=F32)
    inv = ROPE_BASE ** (-jnp.arange(0, MLA_ROPE, 2, dtype=F32) / MLA_ROPE)
    ang = pos[:, None] * inv[None, :]
    cos, sin = jnp.cos(ang), jnp.sin(ang)
    pad = LANES - MLA_NOPE - MLA_ROPE
    cos_t = jnp.concatenate([jnp.ones((s_max, MLA_NOPE), F32), cos, cos, jnp.zeros((s_max, pad), F32)], axis=1)
    sin_t = jnp.concatenate([jnp.zeros((s_max, MLA_NOPE), F32), -sin, sin, jnp.zeros((s_max, pad), F32)], axis=1)
    return cos_t, sin_t


def _swap_halves(w):
    half = w.shape[-1] // 2
    return jnp.concatenate([w[..., half:], w[..., :half]], axis=-1)


def _attn_weights(w_in, w_uq, w_ukv):
    d = w_in.shape[0]
    q_lora = w_uq.shape[0]
    kv_lora = w_ukv.shape[0]
    o1 = q_lora + kv_lora
    o2 = o1 + MLA_ROPE
    pad = LANES - MLA_NOPE - MLA_ROPE
    kpe = w_in[:, o1:o2]
    zl = jnp.zeros((d, MLA_NOPE), w_in.dtype)
    zr = jnp.zeros((d, pad), w_in.dtype)
    w_all = jnp.concatenate([w_in[:, :o1], zl, kpe, zr, zl, _swap_halves(kpe), zr, w_in[:, o2:]], axis=1)

    wq = w_uq.reshape(q_lora, MLA_HEADS, MLA_NOPE + MLA_ROPE)
    zq = jnp.zeros((q_lora, MLA_HEADS, pad), w_uq.dtype)
    wq_p = jnp.concatenate([wq, zq], axis=-1).reshape(q_lora, MLA_HEADS * LANES)
    wq_s = jnp.concatenate([jnp.zeros_like(wq[..., :MLA_NOPE]), _swap_halves(wq[..., MLA_NOPE:]), zq],
                           axis=-1).reshape(q_lora, MLA_HEADS * LANES)

    wkv = w_ukv.reshape(kv_lora, MLA_HEADS, MLA_NOPE + MLA_V)
    zk = jnp.zeros((kv_lora, MLA_HEADS, LANES - MLA_NOPE), w_ukv.dtype)
    wk_p = jnp.concatenate([wkv[..., :MLA_NOPE], zk], axis=-1).reshape(kv_lora, MLA_HEADS * LANES)
    wv = wkv[..., MLA_NOPE:].reshape(kv_lora, MLA_HEADS * MLA_V)
    return tuple(a.astype(BF16) for a in (w_all, wq_p, wq_s, wk_p, wv))


TOKEN_TILE = 512
FFN_TILE = 1024
FFN_CHUNK = 1024
REC_TILE = 512
MLA_TQ = 512


def kernel(x_prompt, x_sample, norm_mix_pre, norm_mix_post, norm_ffn_pre, norm_ffn_post, w_in_attn, q_norm, w_uq, kv_norm, w_ukv, na_rpb, w_out_attn, w_in_rec, conv_w_rec, conv_b_rec, gate_a_w, gate_a_b, gate_x_w, gate_x_b, lru_lambda, w_out_rec, w_ffn_up, conv_w_ffn, conv_b_ffn, w_ffn_down):
    depth = norm_mix_pre.shape[0]
    d_rnn = w_out_rec.shape[1]
    n_mla = MLA_HEADS * MLA_V
    cos_t, sin_t = _rope_tables(max(x_prompt.shape[1], x_sample.shape[1]))

    layers = []
    for li in range(depth):
        j = li // 2
        lw = dict(ffn_up=w_ffn_up[li].astype(BF16), ffn_down=w_ffn_down[li].astype(BF16))
        if li % 2 == 0:
            lw["attn"] = _attn_weights(w_in_attn[j], w_uq[j], w_ukv[j])
            lw["bias"] = _na_bias_table(na_rpb[j])
            w_o = w_out_attn[j].astype(BF16)
            lw["out"] = (w_o[:n_mla], w_o[n_mla:])
        else:
            w_in = w_in_rec[j].astype(BF16)
            lw["rec_in"] = (w_in[:, :d_rnn], w_in[:, d_rnn:])
            lw["gates"] = [(0.5 * jnp.concatenate([gate_a_w[j, e], gate_x_w[j, e]], axis=-1)).astype(BF16)
                           for e in range(2)]
            lw["rec_out"] = w_out_rec[j].astype(BF16)
        layers.append(lw)

    def trunk(x3):
        batch, seq, d = x3.shape
        x = x3.reshape(batch * seq, d)
        tile = min(TOKEN_TILE, seq)
        h = None
        for li, lw in enumerate(layers):
            j = li // 2
            if li % 2 == 0:
                w_all, wq_p, wq_s, wk_p, wv = lw["attn"]
                q, k, v, naq, nak, nav = _attn_in(
                    x, _row(norm_mix_pre[li]), w_all, _row(q_norm[j]), wq_p, wq_s, _row(kv_norm[j]), wk_p, wv,
                    cos_t, sin_t, tile=tile, seq=seq)
                o_mla = _mla(q, k, v, batch=batch, seq=seq, tq=min(MLA_TQ, seq))
                o_na = _na(naq, nak, nav, lw["bias"], batch=batch, seq=seq)
                x, h = _attn_out(o_mla, o_na, lw["out"][0], lw["out"][1], x, _row(norm_mix_post[li]),
                                 _row(norm_ffn_pre[li]), tile=tile)
            else:
                if h is None:
                    h = _prenorm(x, _row(norm_mix_pre[li]), tile=tile)
                rec_tile = min(REC_TILE, seq)
                gate, xc, hf = _rec_fwd(h, lw["rec_in"][0], lw["rec_in"][1], conv_w_rec[j], _row(conv_b_rec[j]),
                                        lw["gates"][0], _row(gate_a_b[j, 0]), _row(gate_x_b[j, 0]),
                                        _row(lru_lambda[j, 0]), tile=rec_tile, seq=seq)
                x, h = _rec_bwd(xc, gate, hf, x, lw["gates"][1], _row(gate_a_b[j, 1]), _row(gate_x_b[j, 1]),
                                _row(lru_lambda[j, 1]), lw["rec_out"], _row(norm_mix_post[li]),
                                _row(norm_ffn_pre[li]), tile=rec_tile, seq=seq)
            g_next = _row(norm_mix_pre[li + 1]) if li + 1 < depth else None
            x, h = _ffn(h, x, lw["ffn_up"], conv_w_ffn[li], _row(conv_b_ffn[li]), lw["ffn_down"],
                        _row(norm_ffn_post[li]), g_next, tile=min(FFN_TILE, seq),
                        fc=min(FFN_CHUNK, lw["ffn_down"].shape[0]), seq=seq)
        return x.reshape(batch, seq, d)

    return trunk(x_prompt), trunk(x_sample)
```

```python
import functools
import math

import numpy as np
import jax
import jax.numpy as jnp
from jax import lax
from jax.experimental import pallas as pl
from jax.experimental.pallas import tpu as pltpu

F32 = jnp.float32
BF16 = jnp.bfloat16

EPS = 1e-6
GRID_W = 64
MLA_HEADS = 8
MLA_NOPE = 64
MLA_ROPE = 32
MLA_V = 64
ROPE_BASE = 10000.0
NA_HEADS = 8
NA_HD = 64
NA_KH = 8
NA_KW = 16
LRU_C = 8.0

LANES = 128
SUB = 8
HALO = 16
NEG = -1e30
VMEM_LIMIT = 56 * 1024 * 1024
LOG2E = math.log2(math.e)


def _rms(x, g):
    return x * lax.rsqrt(jnp.mean(x * x, axis=-1, keepdims=True) + EPS) * g


def _gelu_tanh(x):
    c1 = math.sqrt(2.0 / math.pi)
    k1 = -2.0 * c1 * LOG2E
    k2 = k1 * 0.044715
    e = jnp.exp2(x * ((x * x) * k2 + k1))
    return x / (1.0 + e)


def _dot(a, b):
    return jnp.dot(a, b, preferred_element_type=F32)


def _dot_nt(a, b):
    return lax.dot_general(a, b, (((1,), (1,)), ((), ())), preferred_element_type=F32)


def _params(sem):
    return pltpu.CompilerParams(dimension_semantics=sem, vmem_limit_bytes=VMEM_LIMIT)


def _const_spec(shape):
    nd = len(shape)
    return pl.BlockSpec(shape, lambda *_: (0,) * nd)


def _attn_in_kernel(x_ref, g_ref, win_ref, qn_ref, wq_ref, wqs_ref, kvn_ref, wk_ref, wv_ref,
                    cos_ref, sin_ref,
                    q_ref, k_ref, v_ref, naq_ref, nak_ref, nav_ref, *, q_lora, kv_lora, q_scale, na_scale):
    h = _rms(x_ref[...], g_ref[...]).astype(BF16)
    z = _dot(h, win_ref[...])
    o0 = q_lora
    o1 = o0 + kv_lora
    o2 = o1 + LANES
    o3 = o2 + LANES
    na_w = NA_HEADS * NA_HD
    cos = cos_ref[...]
    sin = sin_ref[...]

    nq = _rms(z[:, :o0], qn_ref[...]).astype(BF16)
    q = _dot(nq, wq_ref[...])
    qs = _dot(nq, wqs_ref[...])
    for hd in range(MLA_HEADS):
        sl = slice(hd * LANES, (hd + 1) * LANES)
        q_ref[:, sl] = ((q[:, sl] * cos + qs[:, sl] * sin) * q_scale).astype(BF16)

    nkv = _rms(z[:, o0:o1], kvn_ref[...]).astype(BF16)
    k = _dot(nkv, wk_ref[...])
    rot = z[:, o1:o2] * cos + z[:, o2:o3] * sin
    for hd in range(MLA_HEADS):
        sl = slice(hd * LANES, (hd + 1) * LANES)
        k_ref[:, sl] = (k[:, sl] + rot).astype(BF16)
    v_ref[...] = _dot(nkv, wv_ref[...]).astype(BF16)

    naq_ref[...] = (z[:, o3:o3 + na_w] * na_scale).astype(BF16)
    nak_ref[...] = z[:, o3 + na_w:o3 + 2 * na_w].astype(BF16)
    nav_ref[...] = z[:, o3 + 2 * na_w:o3 + 3 * na_w].astype(BF16)


def _attn_in(x, g, w_all, qn, wq, wqs, kvn, wk, wv, cos_t, sin_t, *, tile, seq):
    n, d = x.shape
    q_lora = qn.shape[1]
    kv_lora = kvn.shape[1]
    hw = MLA_HEADS * LANES
    vw = MLA_HEADS * MLA_V
    na_w = NA_HEADS * NA_HD
    per_seq = seq // tile

    tok = lambda w: pl.BlockSpec((tile, w), lambda i: (i, 0))
    rope = pl.BlockSpec((tile, LANES), lambda i: (lax.rem(i, per_seq), 0))
    kern = functools.partial(_attn_in_kernel, q_lora=q_lora, kv_lora=kv_lora,
                             q_scale=float((MLA_NOPE + MLA_ROPE) ** -0.5 * LOG2E),
                             na_scale=float(NA_HD ** -0.5 * LOG2E))
    return pl.pallas_call(
        kern,
        grid=(n // tile,),
        in_specs=[tok(d), _const_spec(g.shape), _const_spec(w_all.shape), _const_spec(qn.shape),
                  _const_spec(wq.shape), _const_spec(wqs.shape), _const_spec(kvn.shape),
                  _const_spec(wk.shape), _const_spec(wv.shape), rope, rope],
        out_specs=[tok(hw), tok(hw), tok(vw), tok(na_w), tok(na_w), tok(na_w)],
        out_shape=[jax.ShapeDtypeStruct((n, hw), BF16), jax.ShapeDtypeStruct((n, hw), BF16),
                   jax.ShapeDtypeStruct((n, vw), BF16),
                   jax.ShapeDtypeStruct((n, na_w), BF16), jax.ShapeDtypeStruct((n, na_w), BF16),
                   jax.ShapeDtypeStruct((n, na_w), BF16)],
        compiler_params=_params(("parallel",)),
        name="attn_in",
    )(x, g, w_all, qn, wq, wqs, kvn, wk, wv, cos_t, sin_t)


def _mla_kernel(q_ref, k_ref, v_ref, o_ref):
    tq = q_ref.shape[0]
    seq = k_ref.shape[0]
    heads = [slice(hd * LANES, (hd + 1) * LANES) for hd in range(2)]
    scores = [_dot_nt(q_ref[:, hsl], k_ref[:, hsl]) for hsl in heads]
    probs = [jnp.exp2((s - jnp.max(s, axis=-1, keepdims=True)).astype(BF16)) for s in scores]
    v_ext = jnp.concatenate([v_ref[...], jnp.ones((seq, LANES), BF16)], axis=1)
    accs = [_dot(p, v_ext) for p in probs]
    outs = [acc[:, :LANES] / acc[:, LANES:LANES + 1] for acc in accs]
    lane = lax.broadcasted_iota(jnp.int32, (tq, LANES), 1)
    o_ref[...] = jnp.where(lane < MLA_V, outs[0], outs[1]).astype(BF16)


def _mla(q, k, v, *, batch, seq, tq):
    n = q.shape[0]
    nq = seq // tq
    return pl.pallas_call(
        _mla_kernel,
        grid=(batch, MLA_HEADS // 2, nq),
        in_specs=[pl.BlockSpec((tq, 2 * LANES), lambda b, hp, i: (b * nq + i, hp)),
                  pl.BlockSpec((seq, 2 * LANES), lambda b, hp, i: (b, hp)),
                  pl.BlockSpec((seq, LANES), lambda b, hp, i: (b, hp))],
        out_specs=pl.BlockSpec((tq, LANES), lambda b, hp, i: (b * nq + i, hp)),
        out_shape=jax.ShapeDtypeStruct((n, MLA_HEADS * MLA_V), BF16),
        compiler_params=_params(("parallel", "parallel", "arbitrary")),
        name="mla_attn",
    )(q, k, v)


NA_GROUP = 16


def _na_kernel(q_ref, k_ref, v_ref, bias_ref, o_ref, *, rows):
    win = NA_KH * GRID_W
    lane_q = lax.broadcasted_iota(jnp.int32, (GRID_W, LANES), 1)
    ones = jnp.ones((win, LANES), BF16)
    group = math.gcd(NA_GROUP, rows)

    def body(g, _):
        q0s, k0s, scores = [], [], []
        for j in range(group):
            r = g * group + j
            rs = jnp.clip(r - NA_KH // 2, 0, rows - NA_KH)
            q0 = pl.multiple_of(r * GRID_W, GRID_W)
            k0 = pl.multiple_of(rs * GRID_W, GRID_W)
            qr = q_ref[pl.ds(q0, GRID_W), :]
            zero = jnp.zeros_like(qr)
            q2 = jnp.concatenate([jnp.where(lane_q < NA_HD, qr, zero), jnp.where(lane_q >= NA_HD, qr, zero)],
                                 axis=0)
            scores.append(_dot_nt(q2, k_ref[pl.ds(k0, win), :]) + bias_ref[r - rs])
            q0s.append(q0)
            k0s.append(k0)
        probs = [jnp.exp2((s - jnp.max(s, axis=-1, keepdims=True)).astype(BF16)) for s in scores]
        accs = [_dot(p, jnp.concatenate([v_ref[pl.ds(k0, win), :], ones], axis=1)) for p, k0 in zip(probs, k0s)]
        for q0, acc in zip(q0s, accs):
            o2 = acc[:, :LANES] / acc[:, LANES:LANES + 1]
            o_ref[pl.ds(q0, GRID_W), :] = jnp.where(lane_q < NA_HD, o2[:GRID_W], o2[GRID_W:]).astype(BF16)
        return 0

    lax.fori_loop(0, rows // group, body, 0)


def _na(q, k, v, bias, *, batch, seq):
    n = q.shape[0]
    rows = seq // GRID_W
    blk = pl.BlockSpec((seq, LANES), lambda b, hp: (b, hp))
    return pl.pallas_call(
        functools.partial(_na_kernel, rows=rows),
        grid=(batch, NA_HEADS // 2),
        in_specs=[blk, blk, blk, pl.BlockSpec((None,) + bias.shape[1:], lambda b, hp: (hp, 0, 0, 0))],
        out_specs=blk,
        out_shape=jax.ShapeDtypeStruct((n, NA_HEADS * NA_HD), BF16),
        compiler_params=_params(("parallel", "parallel")),
        name="na_attn",
    )(q, k, v, bias)


def _na_bias_table(rpb):
    n_rel = 2 * NA_KW - 1
    period = 2 * GRID_W
    left = GRID_W - NA_KW
    p = jnp.pad(rpb.astype(F32), ((0, 0), (0, 0), (left, period - n_rel - left)))
    lead = p.shape[:-1]
    flat = jnp.broadcast_to(p[..., None, :], lead + (GRID_W, period)).reshape(lead + (GRID_W * period,))
    skew = flat[..., :GRID_W * (period - 1)].reshape(lead + (GRID_W, period - 1))
    by_col = skew[..., GRID_W - 1:2 * GRID_W - 1]
    cols = np.arange(GRID_W)
    col_start = np.clip(cols - NA_KW // 2, 0, GRID_W - NA_KW)
    kc = np.arange(GRID_W)
    in_win = (kc[None, :] >= col_start[:, None]) & (kc[None, :] < col_start[:, None] + NA_KW)
    by_col = jnp.where(jnp.asarray(in_win)[None, None], by_col * LOG2E, NEG)
    t = jnp.stack([by_col[:, NA_KH - 1 - dl:2 * NA_KH - 1 - dl] for dl in range(NA_KH)], axis=1)
    t = jnp.swapaxes(t, 2, 3).reshape(rpb.shape[0] // 2, 2, NA_KH, GRID_W, NA_KH * GRID_W)
    return jnp.swapaxes(t, 1, 2).reshape(rpb.shape[0] // 2, NA_KH, 2 * GRID_W, NA_KH * GRID_W)


def _attn_out_kernel(om_ref, on_ref, wm_ref, wn_ref, x_ref, gpost_ref, gnext_ref, xo_ref, ho_ref):
    m = _dot(om_ref[...], wm_ref[...]) + _dot(on_ref[...], wn_ref[...])
    xo = x_ref[...] + _rms(m, gpost_ref[...])
    xo_ref[...] = xo
    ho_ref[...] = _rms(xo, gnext_ref[...]).astype(BF16)


def _attn_out(o_mla, o_na, w_m, w_n, x, g_post, g_next, *, tile):
    n, d = x.shape
    tok = lambda w: pl.BlockSpec((tile, w), lambda i: (i, 0))
    return pl.pallas_call(
        _attn_out_kernel,
        grid=(n // tile,),
        in_specs=[tok(o_mla.shape[1]), tok(o_na.shape[1]), _const_spec(w_m.shape), _const_spec(w_n.shape),
                  tok(d), _const_spec(g_post.shape), _const_spec(g_next.shape)],
        out_specs=[tok(d), tok(d)],
        out_shape=[jax.ShapeDtypeStruct((n, d), F32), jax.ShapeDtypeStruct((n, d), BF16)],
        compiler_params=_params(("parallel",)),
        name="attn_out",
    )(o_mla, o_na, w_m, w_n, x, g_post, g_next)


def _halo_specs(tile, d, n):
    per = tile // HALO
    last = n // HALO - 1
    return [pl.BlockSpec((HALO, d), lambda i, *_: (jnp.maximum(i * per - 1, 0), 0)),
            pl.BlockSpec((tile, d), lambda i, *_: (i, 0)),
            pl.BlockSpec((HALO, d), lambda i, *_: (jnp.minimum((i + 1) * per, last), 0))]


def _fill_halo(hh_ref, hp_ref, hm_ref, hn_ref, pos0, tile, seq):
    zero = jnp.zeros(hp_ref.shape, hp_ref.dtype)
    hh_ref[0:HALO, :] = jnp.where(pos0 == 0, zero, hp_ref[...])
    hh_ref[HALO:HALO + tile, :] = hm_ref[...]
    hh_ref[HALO + tile:, :] = jnp.where(pos0 + tile == seq, zero, hn_ref[...])


def _dwconv(u, w, b, tile, left, chained_rolls=False):
    taps = w.shape[0]
    if chained_rolls:
        n = u.shape[0]
        shifted = {0: u}
        for d in range(1, taps - left):
            shifted[d] = pltpu.roll(shifted[d - 1], n - 1, axis=0)
        for d in range(1, left + 1):
            shifted[-d] = pltpu.roll(shifted[1 - d], 1, axis=0)
        y = b
        for kk in range(taps):
            y = y + w[kk:kk + 1, :] * shifted[kk - left][HALO:HALO + tile, :]
        return y
    y = b
    for kk in range(taps):
        s = HALO - left + kk
        y = y + w[kk:kk + 1, :] * u[s:s + tile, :]
    return y


def _slab_conv3(u, w, b, tile):
    s = tile // SUB
    slabs = [u[r * s:(r + 1) * s, :] for r in range(SUB)]
    row = lax.broadcasted_iota(jnp.int32, slabs[0].shape, 0)
    before = jnp.where(row == 0, u[tile:tile + 1, :], pltpu.roll(slabs[SUB - 1], 1, axis=0))
    after = jnp.where(row == s - 1, u[tile + 1:tile + 2, :], pltpu.roll(slabs[0], s - 1, axis=0))
    out = []
    for r in range(SUB):
        left = slabs[r - 1] if r > 0 else before
        right = slabs[r + 1] if r < SUB - 1 else after
        out.append(b + w[0:1, :] * left + w[1:2, :] * slabs[r] + w[2:3, :] * right)
    return jnp.concatenate(out, axis=0)


def _ffn_kernel(*refs, tile, seq, emit_next):
    if emit_next:
        (hp_ref, hm_ref, hn_ref, x_ref, wg_ref, wv_ref, cwg_ref, cwv_ref, cbg_ref, cbv_ref, wd_ref,
         gpost_ref, gnext_ref, xo_ref, ho_ref, hh_ref, acc_ref, nat_ref) = refs
    else:
        (hp_ref, hm_ref, hn_ref, x_ref, wg_ref, wv_ref, cwg_ref, cwv_ref, cbg_ref, cbv_ref, wd_ref,
         gpost_ref, xo_ref, hh_ref, acc_ref, nat_ref) = refs
    i = pl.program_id(0)
    c = pl.program_id(1)
    s = tile // SUB
    ng = nat_ref.shape[0]

    @pl.when(c == 0)
    def _():
        pos0 = lax.rem(i * tile, seq)
        for k in range(ng):
            nat_ref[k] = hm_ref[:, k * LANES:(k + 1) * LANES].astype(F32)
        for r in range(SUB):
            hh_ref[r * s:(r + 1) * s, :] = jnp.concatenate(
                [nat_ref[k, pl.ds(r, s, stride=SUB), :] for k in range(ng)], axis=1).astype(BF16)
        zero = jnp.zeros((1, hm_ref.shape[1]), F32)
        prev = jnp.where(pos0 == 0, zero, hp_ref[HALO - 1:HALO, :].astype(F32))
        nxt = jnp.where(pos0 + tile == seq, zero, hn_ref[0:1, :].astype(F32))
        tail = jnp.concatenate([prev, nxt, jnp.zeros((HALO - 2, hm_ref.shape[1]), F32)], axis=0)
        hh_ref[tile:, :] = tail.astype(BF16)
        acc_ref[...] = jnp.zeros(acc_ref.shape, F32)

    hh = hh_ref[...]
    g = _slab_conv3(_dot(hh, wg_ref[...]), cwg_ref[...], cbg_ref[...], tile)
    val = _slab_conv3(_dot(hh, wv_ref[...]), cwv_ref[...], cbv_ref[...], tile)
    a = (_gelu_tanh(g) * val).astype(BF16)
    acc_ref[...] += _dot(a, wd_ref[...])

    @pl.when(c == pl.num_programs(1) - 1)
    def _():
        f = _rms(acc_ref[...], gpost_ref[...])
        for r in range(SUB):
            for k in range(ng):
                nat_ref[k, pl.ds(r, s, stride=SUB), :] = f[r * s:(r + 1) * s, k * LANES:(k + 1) * LANES]
        xo = x_ref[...] + jnp.concatenate([nat_ref[k] for k in range(ng)], axis=1)
        xo_ref[...] = xo
        if emit_next:
            ho_ref[...] = _rms(xo, gnext_ref[...]).astype(BF16)


def _ffn(h, x, w_up, conv_w, conv_b, w_down, g_post, g_next, *, tile, fc, seq):
    n, d = x.shape
    d_ff = w_down.shape[0]
    nc = d_ff // fc
    kk = conv_w.shape[0]
    assert kk == 3 and d % LANES == 0 and tile % (SUB * SUB) == 0
    emit_next = g_next is not None
    tok = pl.BlockSpec((tile, d), lambda i, c: (i, 0))
    in_specs = _halo_specs(tile, d, n) + [
        tok,
        pl.BlockSpec((d, fc), lambda i, c: (0, c)),
        pl.BlockSpec((d, fc), lambda i, c: (0, nc + c)),
        pl.BlockSpec((kk, fc), lambda i, c: (0, c)),
        pl.BlockSpec((kk, fc), lambda i, c: (0, nc + c)),
        pl.BlockSpec((1, fc), lambda i, c: (0, c)),
        pl.BlockSpec((1, fc), lambda i, c: (0, nc + c)),
        pl.BlockSpec((fc, d), lambda i, c: (c, 0)),
        _const_spec(g_post.shape)]
    args = [h, h, h, x, w_up, w_up, conv_w, conv_w, conv_b, conv_b, w_down, g_post]
    out_specs = [tok]
    out_shape = [jax.ShapeDtypeStruct((n, d), F32)]
    if emit_next:
        in_specs.append(_const_spec(g_next.shape))
        args.append(g_next)
        out_specs.append(tok)
        out_shape.append(jax.ShapeDtypeStruct((n, d), BF16))
    res = pl.pallas_call(
        functools.partial(_ffn_kernel, tile=tile, seq=seq, emit_next=emit_next),
        grid=(n // tile, nc),
        in_specs=in_specs,
        out_specs=out_specs,
        out_shape=out_shape,
        scratch_shapes=[pltpu.VMEM((tile + HALO, d), BF16), pltpu.VMEM((tile, d), F32),
                        pltpu.VMEM((d // LANES, tile, LANES), F32)],
        compiler_params=_params(("parallel", "arbitrary")),
        name="conv_ffn",
    )(*args)
    return (res[0], res[1]) if emit_next else (res[0], None)


def _lru_coeffs(xc, xc_bf, gw_ref, gab_ref, gxb_ref, lam_ref):
    nb, bw, _ = gw_ref.shape
    neg = -lam_ref[...]
    sp = jnp.maximum(neg, 0.0) + jnp.log1p(jnp.exp(-jnp.abs(neg)))
    half_c = (-0.5 * LRU_C) * sp
    half_gab = 0.5 * gab_ref[...]
    half_gxb = 0.5 * gxb_ref[...]
    a_parts, b_parts = [], []
    for blk in range(nb):
        sl = slice(blk * bw, (blk + 1) * bw)
        rg = _dot(xc_bf[:, sl], gw_ref[blk])
        hc = half_c[:, sl]
        log_a = hc * jnp.tanh(rg[:, :bw] + half_gab[:, sl]) + hc
        hx = 0.5 * xc[:, sl]
        gated_x = hx * jnp.tanh(rg[:, bw:] + half_gxb[:, sl]) + hx
        t = jnp.tanh(log_a)
        one_minus_a2 = (-2.0 * t) / (1.0 - t)
        root = jnp.where(one_minus_a2 > 0.0, one_minus_a2 * lax.rsqrt(one_minus_a2), 0.0)
        a_parts.append(jnp.exp2(log_a * LOG2E))
        b_parts.append(root * gated_x)
    return jnp.concatenate(a_parts, axis=1), jnp.concatenate(b_parts, axis=1)


def _scan_tile(a_ref, b_ref, carry_ref, *, reverse):
    t, ch = a_ref.shape
    n_chunks = t // SUB
    row = lax.broadcasted_iota(jnp.int32, (SUB, ch), 0)

    def body(j, carry):
        jj = (n_chunks - 1 - j) if reverse else j
        r0 = pl.multiple_of(jj * SUB, SUB)
        a = a_ref[pl.ds(r0, SUB), :]
        b = b_ref[pl.ds(r0, SUB), :]
        for dist in (1, 2, 4):
            shift = (SUB - dist) if reverse else dist
            a_sh = pltpu.roll(a, shift, axis=0)
            b_sh = pltpu.roll(b, shift, axis=0)
            valid = (row < SUB - dist) if reverse else (row >= dist)
            b = jnp.where(valid, a * b_sh + b, b)
            a = jnp.where(valid, a * a_sh, a)
        hcur = a * carry + b
        b_ref[pl.ds(r0, SUB), :] = hcur
        edge = hcur[0:1, :] if reverse else hcur[SUB - 1:SUB, :]
        return jnp.broadcast_to(edge, (SUB, ch))

    carry_ref[...] = lax.fori_loop(0, n_chunks, body, carry_ref[...], unroll=2)


def _rec_fwd_kernel(hp_ref, hm_ref, hn_ref, wg_ref, wx_ref, cw_ref, cb_ref, gw_ref, gab_ref, gxb_ref,
                    lam_ref, gate_ref, xc_ref, hf_ref, hh_ref, a_ref, b_ref, carry_ref, *, tile, seq):
    pos0 = lax.rem(pl.program_id(0) * tile, seq)
    _fill_halo(hh_ref, hp_ref, hm_ref, hn_ref, pos0, tile, seq)

    @pl.when(pos0 == 0)
    def _():
        carry_ref[...] = jnp.zeros(carry_ref.shape, F32)

    gate_ref[...] = _dot(hm_ref[...], wg_ref[...]).astype(BF16)
    left = (cw_ref.shape[0] - 1) // 2
    xc = _dwconv(_dot(hh_ref[...], wx_ref[...]), cw_ref[...], cb_ref[...], tile, left, chained_rolls=True)
    xc_bf = xc.astype(BF16)
    xc_ref[...] = xc_bf
    a, b = _lru_coeffs(xc, xc_bf, gw_ref, gab_ref, gxb_ref, lam_ref)
    a_ref[...] = a
    b_ref[...] = b
    _scan_tile(a_ref, b_ref, carry_ref, reverse=False)
    hf_ref[...] = b_ref[...].astype(BF16)


def _rec_fwd(h, w_g, w_x, conv_w, conv_b, gw, gab, gxb, lam, *, tile, seq):
    n, d = h.shape
    d_rnn = w_x.shape[1]
    tok = pl.BlockSpec((tile, d_rnn), lambda i: (i, 0))
    in_specs = _halo_specs(tile, d, n) + [_const_spec(a.shape) for a in
                                          (w_g, w_x, conv_w, conv_b, gw, gab, gxb, lam)]
    out = jax.ShapeDtypeStruct((n, d_rnn), BF16)
    return pl.pallas_call(
        functools.partial(_rec_fwd_kernel, tile=tile, seq=seq),
        grid=(n // tile,),
        in_specs=in_specs,
        out_specs=[tok, tok, tok],
        out_shape=[out, out, out],
        scratch_shapes=[pltpu.VMEM((tile + 2 * HALO, d), BF16), pltpu.VMEM((tile, d_rnn), F32),
                        pltpu.VMEM((tile, d_rnn), F32), pltpu.VMEM((SUB, d_rnn), F32)],
        compiler_params=_params(("arbitrary",)),
        name="rglru_fwd",
    )(h, h, h, w_g, w_x, conv_w, conv_b, gw, gab, gxb, lam)


def _rec_bwd_kernel(xc_ref, gate_ref, hf_ref, x_ref, gw_ref, gab_ref, gxb_ref, lam_ref, wo_ref,
                    gpost_ref, gnext_ref, xo_ref, ho_ref, a_ref, b_ref, carry_ref, *, tile, seq):
    i = pl.num_programs(0) - 1 - pl.program_id(0)
    pos0 = lax.rem(i * tile, seq)

    @pl.when(pos0 + tile == seq)
    def _():
        carry_ref[...] = jnp.zeros(carry_ref.shape, F32)

    xc_bf = xc_ref[...]
    a, b = _lru_coeffs(xc_bf.astype(F32), xc_bf, gw_ref, gab_ref, gxb_ref, lam_ref)
    a_ref[...] = a
    b_ref[...] = b
    _scan_tile(a_ref, b_ref, carry_ref, reverse=True)
    y = (hf_ref[...].astype(F32) + b_ref[...]) * _gelu_tanh(gate_ref[...].astype(F32))
    m = _dot(y.astype(BF16), wo_ref[...])
    xo = x_ref[...] + _rms(m, gpost_ref[...])
    xo_ref[...] = xo
    ho_ref[...] = _rms(xo, gnext_ref[...]).astype(BF16)


def _rec_bwd(xc, gate, hf, x, gw, gab, gxb, lam, w_out, g_post, g_next, *, tile, seq):
    n, d = x.shape
    d_rnn = xc.shape[1]
    nt = n // tile
    rtok = lambda w: pl.BlockSpec((tile, w), lambda i: (nt - 1 - i, 0))
    in_specs = [rtok(d_rnn), rtok(d_rnn), rtok(d_rnn), rtok(d)] + [
        _const_spec(a.shape) for a in (gw, gab, gxb, lam, w_out, g_post, g_next)]
    return pl.pallas_call(
        functools.partial(_rec_bwd_kernel, tile=tile, seq=seq),
        grid=(nt,),
        in_specs=in_specs,
        out_specs=[rtok(d), rtok(d)],
        out_shape=[jax.ShapeDtypeStruct((n, d), F32), jax.ShapeDtypeStruct((n, d), BF16)],
        scratch_shapes=[pltpu.VMEM((tile, d_rnn), F32), pltpu.VMEM((tile, d_rnn), F32),
                        pltpu.VMEM((SUB, d_rnn), F32)],
        compiler_params=_params(("arbitrary",)),
        name="rglru_bwd",
    )(xc, gate, hf, x, gw, gab, gxb, lam, w_out, g_post, g_next)


def _prenorm_kernel(x_ref, g_ref, h_ref):
    h_ref[...] = _rms(x_ref[...], g_ref[...]).astype(BF16)


def _prenorm(x, g, *, tile):
    n, d = x.shape
    tok = pl.BlockSpec((tile, d), lambda i: (i, 0))
    return pl.pallas_call(
        _prenorm_kernel, grid=(n // tile,), in_specs=[tok, _const_spec(g.shape)], out_specs=tok,
        out_shape=jax.ShapeDtypeStruct((n, d), BF16), compiler_params=_params(("parallel",)),
        name="prenorm",
    )(x, g)


def _row(v):
    return v.reshape(1, -1).astype(F32)


def _rope_tables(s_max):
    pos = jnp.arange(s_max, dtype=F32)
    inv = ROPE_BASE ** (-jnp.arange(0, MLA_ROPE, 2, dtype=F32) / MLA_ROPE)
    ang = pos[:, None] * inv[None, :]
    cos, sin = jnp.cos(ang), jnp.sin(ang)
    pad = LANES - MLA_NOPE - MLA_ROPE
    cos_t = jnp.concatenate([jnp.ones((s_max, MLA_NOPE), F32), cos, cos, jnp.zeros((s_max, pad), F32)], axis=1)
    sin_t = jnp.concatenate([jnp.zeros((s_max, MLA_NOPE), F32), -sin, sin, jnp.zeros((s_max, pad), F32)], axis=1)
    return cos_t, sin_t


def _swap_halves(w):
    half = w.shape[-1] // 2
    return jnp.concatenate([w[..., half:], w[..., :half]], axis=-1)


def _attn_weights(w_in, w_uq, w_ukv):
    d = w_in.shape[0]
    q_lora = w_uq.shape[0]
    kv_lora = w_ukv.shape[0]
    o1 = q_lora + kv_lora
    o2 = o1 + MLA_ROPE
    pad = LANES - MLA_NOPE - MLA_ROPE
    kpe = w_in[:, o1:o2]
    zl = jnp.zeros((d, MLA_NOPE), w_in.dtype)
    zr = jnp.zeros((d, pad), w_in.dtype)
    w_all = jnp.concatenate([w_in[:, :o1], zl, kpe, zr, zl, _swap_halves(kpe), zr, w_in[:, o2:]], axis=1)

    wq = w_uq.reshape(q_lora, MLA_HEADS, MLA_NOPE + MLA_ROPE)
    zq = jnp.zeros((q_lora, MLA_HEADS, pad), w_uq.dtype)
    wq_p = jnp.concatenate([wq, zq], axis=-1).reshape(q_lora, MLA_HEADS * LANES)
    wq_s = jnp.concatenate([jnp.zeros_like(wq[..., :MLA_NOPE]), _swap_halves(wq[..., MLA_NOPE:]), zq],
                           axis=-1).reshape(q_lora, MLA_HEADS * LANES)

    wkv = w_ukv.reshape(kv_lora, MLA_HEADS, MLA_NOPE + MLA_V)
    zk = jnp.zeros((kv_lora, MLA_HEADS, LANES - MLA_NOPE), w_ukv.dtype)
    wk_p = jnp.concatenate([wkv[..., :MLA_NOPE], zk], axis=-1).reshape(kv_lora, MLA_HEADS * LANES)
    wv = wkv[..., MLA_NOPE:].reshape(kv_lora, MLA_HEADS * MLA_V)
    return tuple(a.astype(BF16) for a in (w_all, wq_p, wq_s, wk_p, wv))


TOKEN_TILE = 512
FFN_TILE = 1024
FFN_CHUNK = 1024
REC_TILE = 512
MLA_TQ = 512


def kernel(x_prompt, x_sample, norm_mix_pre, norm_mix_post, norm_ffn_pre, norm_ffn_post, w_in_attn, q_norm, w_uq, kv_norm, w_ukv, na_rpb, w_out_attn, w_in_rec, conv_w_rec, conv_b_rec, gate_a_w, gate_a_b, gate_x_w, gate_x_b, lru_lambda, w_out_rec, w_ffn_up, conv_w_ffn, conv_b_ffn, w_ffn_down):
    depth = norm_mix_pre.shape[0]
    d_rnn = w_out_rec.shape[1]
    n_mla = MLA_HEADS * MLA_V
    cos_t, sin_t = _rope_tables(max(x_prompt.shape[1], x_sample.shape[1]))

    layers = []
    for li in range(depth):
        j = li // 2
        lw = dict(ffn_up=w_ffn_up[li].astype(BF16), ffn_down=w_ffn_down[li].astype(BF16))
        if li % 2 == 0:
            lw["attn"] = _attn_weights(w_in_attn[j], w_uq[j], w_ukv[j])
            lw["bias"] = _na_bias_table(na_rpb[j])
            w_o = w_out_attn[j].astype(BF16)
            lw["out"] = (w_o[:n_mla], w_o[n_mla:])
        else:
            w_in = w_in_rec[j].astype(BF16)
            lw["rec_in"] = (w_in[:, :d_rnn], w_in[:, d_rnn:])
            lw["gates"] = [(0.5 * jnp.concatenate([gate_a_w[j, e], gate_x_w[j, e]], axis=-1)).astype(BF16)
                           for e in range(2)]
            lw["rec_out"] = w_out_rec[j].astype(BF16)
        layers.append(lw)

    def trunk(x3):
        batch, seq, d = x3.shape
        x = x3.reshape(batch * seq, d)
        tile = min(TOKEN_TILE, seq)
        h = None
        for li, lw in enumerate(layers):
            j = li // 2
            if li % 2 == 0:
                w_all, wq_p, wq_s, wk_p, wv = lw["attn"]
                q, k, v, naq, nak, nav = _attn_in(
                    x, _row(norm_mix_pre[li]), w_all, _row(q_norm[j]), wq_p, wq_s, _row(kv_norm[j]), wk_p, wv,
                    cos_t, sin_t, tile=tile, seq=seq)
                o_mla = _mla(q, k, v, batch=batch, seq=seq, tq=min(MLA_TQ, seq))
                o_na = _na(naq, nak, nav, lw["bias"], batch=batch, seq=seq)
                x, h = _attn_out(o_mla, o_na, lw["out"][0], lw["out"][1], x, _row(norm_mix_post[li]),
                                 _row(norm_ffn_pre[li]), tile=tile)
            else:
                if h is None:
                    h = _prenorm(x, _row(norm_mix_pre[li]), tile=tile)
                rec_tile = min(REC_TILE, seq)
                gate, xc, hf = _rec_fwd(h, lw["rec_in"][0], lw["rec_in"][1], conv_w_rec[j], _row(conv_b_rec[j]),
                                        lw["gates"][0], _row(gate_a_b[j, 0]), _row(gate_x_b[j, 0]),
                                        _row(lru_lambda[j, 0]), tile=rec_tile, seq=seq)
                x, h = _rec_bwd(xc, gate, hf, x, lw["gates"][1], _row(gate_a_b[j, 1]), _row(gate_x_b[j, 1]),
                                _row(lru_lambda[j, 1]), lw["rec_out"], _row(norm_mix_post[li]),
                                _row(norm_ffn_pre[li]), tile=rec_tile, seq=seq)
            g_next = _row(norm_mix_pre[li + 1]) if li + 1 < depth else None
            x, h = _ffn(h, x, lw["ffn_up"], conv_w_ffn[li], _row(conv_b_ffn[li]), lw["ffn_down"],
                        _row(norm_ffn_post[li]), g_next, tile=min(FFN_TILE, seq),
                        fc=min(FFN_CHUNK, lw["ffn_down"].shape[0]), seq=seq)
        return x.reshape(batch, seq, d)

    return trunk(x_prompt), trunk(x_sample)
```

```python
import functools
import math

import numpy as np
import jax
import jax.numpy as jnp
from jax import lax
from jax.experimental import pallas as pl
from jax.experimental.pallas import tpu as pltpu

F32 = jnp.float32
BF16 = jnp.bfloat16

EPS = 1e-6
GRID_W = 64
MLA_HEADS = 8
MLA_NOPE = 64
MLA_ROPE = 32
MLA_V = 64
ROPE_BASE = 10000.0
NA_HEADS = 8
NA_HD = 64
NA_KH = 8
NA_KW = 16
LRU_C = 8.0

LANES = 128
SUB = 8
HALO = 16
NEG = -1e30
VMEM_LIMIT = 56 * 1024 * 1024
LOG2E = math.log2(math.e)


def _rms(x, g):
    return x * lax.rsqrt(jnp.mean(x * x, axis=-1, keepdims=True) + EPS) * g


def _gelu_tanh(x):
    c1 = math.sqrt(2.0 / math.pi)
    k1 = -2.0 * c1 * LOG2E
    k2 = k1 * 0.044715
    e = jnp.exp2(x * ((x * x) * k2 + k1))
    return x / (1.0 + e)


def _dot(a, b):
    return jnp.dot(a, b, preferred_element_type=F32)


def _dot_nt(a, b):
    return lax.dot_general(a, b, (((1,), (1,)), ((), ())), preferred_element_type=F32)


def _params(sem):
    return pltpu.CompilerParams(dimension_semantics=sem, vmem_limit_bytes=VMEM_LIMIT)


def _const_spec(shape):
    nd = len(shape)
    return pl.BlockSpec(shape, lambda *_: (0,) * nd)


def _rope(x, cos, sin):
    half = MLA_ROPE // 2
    lane = lax.broadcasted_iota(jnp.int32, x.shape, 1)
    partner = jnp.where(lane < MLA_NOPE + half, pltpu.roll(x, LANES - half, axis=1), pltpu.roll(x, half, axis=1))
    return x * cos + partner * sin


def _attn_in_kernel(x_ref, g_ref, win_ref, qn_ref, wq_ref, kvn_ref, wk_ref, wv_ref,
                    cos_ref, sin_ref,
                    q_ref, k_ref, v_ref, naq_ref, nak_ref, nav_ref, *, q_lora, kv_lora, q_scale, na_scale):
    h = _rms(x_ref[...], g_ref[...]).astype(BF16)
    z = _dot(h, win_ref[...])
    o0 = q_lora
    o1 = o0 + kv_lora
    o3 = o1 + LANES
    na_w = NA_HEADS * NA_HD
    cos = cos_ref[...]
    sin = sin_ref[...]

    nq = _rms(z[:, :o0], qn_ref[...]).astype(BF16)
    q = _dot(nq, wq_ref[...])
    for hd in range(MLA_HEADS):
        sl = slice(hd * LANES, (hd + 1) * LANES)
        q_ref[:, sl] = (_rope(q[:, sl], cos, sin) * q_scale).astype(BF16)

    nkv = _rms(z[:, o0:o1], kvn_ref[...]).astype(BF16)
    k = _dot(nkv, wk_ref[...])
    rot = _rope(z[:, o1:o3], cos, sin)
    for hd in range(MLA_HEADS):
        sl = slice(hd * LANES, (hd + 1) * LANES)
        k_ref[:, sl] = (k[:, sl] + rot).astype(BF16)
    v_ref[...] = _dot(nkv, wv_ref[...]).astype(BF16)

    naq_ref[...] = (z[:, o3:o3 + na_w] * na_scale).astype(BF16)
    nak_ref[...] = z[:, o3 + na_w:o3 + 2 * na_w].astype(BF16)
    nav_ref[...] = z[:, o3 + 2 * na_w:o3 + 3 * na_w].astype(BF16)


def _attn_in(x, g, w_all, qn, wq, kvn, wk, wv, cos_t, sin_t, *, tile, seq):
    n, d = x.shape
    q_lora = qn.shape[1]
    kv_lora = kvn.shape[1]
    hw = MLA_HEADS * LANES
    vw = MLA_HEADS * MLA_V
    na_w = NA_HEADS * NA_HD
    per_seq = seq // tile

    tok = lambda w: pl.BlockSpec((tile, w), lambda i: (i, 0))
    rope = pl.BlockSpec((tile, LANES), lambda i: (lax.rem(i, per_seq), 0))
    kern = functools.partial(_attn_in_kernel, q_lora=q_lora, kv_lora=kv_lora,
                             q_scale=float((MLA_NOPE + MLA_ROPE) ** -0.5 * LOG2E),
                             na_scale=float(NA_HD ** -0.5 * LOG2E))
    return pl.pallas_call(
        kern,
        grid=(n // tile,),
        in_specs=[tok(d), _const_spec(g.shape), _const_spec(w_all.shape), _const_spec(qn.shape),
                  _const_spec(wq.shape), _const_spec(kvn.shape),
                  _const_spec(wk.shape), _const_spec(wv.shape), rope, rope],
        out_specs=[tok(hw), tok(hw), tok(vw), tok(na_w), tok(na_w), tok(na_w)],
        out_shape=[jax.ShapeDtypeStruct((n, hw), BF16), jax.ShapeDtypeStruct((n, hw), BF16),
                   jax.ShapeDtypeStruct((n, vw), BF16),
                   jax.ShapeDtypeStruct((n, na_w), BF16), jax.ShapeDtypeStruct((n, na_w), BF16),
                   jax.ShapeDtypeStruct((n, na_w), BF16)],
        compiler_params=_params(("parallel",)),
        name="attn_in",
    )(x, g, w_all, qn, wq, kvn, wk, wv, cos_t, sin_t)


def _mla_kernel(q_ref, k_ref, v_ref, o_ref):
    tq = q_ref.shape[0]
    seq = k_ref.shape[0]
    heads = [slice(hd * LANES, (hd + 1) * LANES) for hd in range(2)]
    scores = [_dot_nt(q_ref[:, hsl], k_ref[:, hsl]) for hsl in heads]
    probs = [jnp.exp2((s - jnp.max(s, axis=-1, keepdims=True)).astype(BF16)) for s in scores]
    v_ext = jnp.concatenate([v_ref[...], jnp.ones((seq, LANES), BF16)], axis=1)
    accs = [_dot(p, v_ext) for p in probs]
    outs = [acc[:, :LANES] / acc[:, LANES:LANES + 1] for acc in accs]
    lane = lax.broadcasted_iota(jnp.int32, (tq, LANES), 1)
    o_ref[...] = jnp.where(lane < MLA_V, outs[0], outs[1]).astype(BF16)


def _mla(q, k, v, *, batch, seq, tq):
    n = q.shape[0]
    nq = seq // tq
    return pl.pallas_call(
        _mla_kernel,
        grid=(batch, MLA_HEADS // 2, nq),
        in_specs=[pl.BlockSpec((tq, 2 * LANES), lambda b, hp, i: (b * nq + i, hp)),
                  pl.BlockSpec((seq, 2 * LANES), lambda b, hp, i: (b, hp)),
                  pl.BlockSpec((seq, LANES), lambda b, hp, i: (b, hp))],
        out_specs=pl.BlockSpec((tq, LANES), lambda b, hp, i: (b * nq + i, hp)),
        out_shape=jax.ShapeDtypeStruct((n, MLA_HEADS * MLA_V), BF16),
        compiler_params=_params(("parallel", "parallel", "arbitrary")),
        name="mla_attn",
    )(q, k, v)


NA_GROUP = 16


def _na_kernel(q_ref, k_ref, v_ref, bias_ref, o_ref, *, rows):
    win = NA_KH * GRID_W
    lane_q = lax.broadcasted_iota(jnp.int32, (GRID_W, LANES), 1)
    ones = jnp.ones((win, LANES), BF16)
    group = math.gcd(NA_GROUP, rows)

    def body(g, _):
        q0s, k0s, scores = [], [], []
        for j in range(group):
            r = g * group + j
            rs = jnp.clip(r - NA_KH // 2, 0, rows - NA_KH)
            q0 = pl.multiple_of(r * GRID_W, GRID_W)
            k0 = pl.multiple_of(rs * GRID_W, GRID_W)
            qr = q_ref[pl.ds(q0, GRID_W), :]
            zero = jnp.zeros_like(qr)
            q2 = jnp.concatenate([jnp.where(lane_q < NA_HD, qr, zero), jnp.where(lane_q >= NA_HD, qr, zero)],
                                 axis=0)
            scores.append(_dot_nt(q2, k_ref[pl.ds(k0, win), :]) + bias_ref[r - rs])
            q0s.append(q0)
            k0s.append(k0)
        probs = [jnp.exp2((s - jnp.max(s, axis=-1, keepdims=True)).astype(BF16)) for s in scores]
        accs = [_dot(p, jnp.concatenate([v_ref[pl.ds(k0, win), :], ones], axis=1)) for p, k0 in zip(probs, k0s)]
        for q0, acc in zip(q0s, accs):
            o2 = acc[:, :LANES] / acc[:, LANES:LANES + 1]
            o_ref[pl.ds(q0, GRID_W), :] = jnp.where(lane_q < NA_HD, o2[:GRID_W], o2[GRID_W:]).astype(BF16)
        return 0

    lax.fori_loop(0, rows // group, body, 0)


def _na(q, k, v, bias, *, batch, seq):
    n = q.shape[0]
    rows = seq // GRID_W
    blk = pl.BlockSpec((seq, LANES), lambda b, hp: (b, hp))
    return pl.pallas_call(
        functools.partial(_na_kernel, rows=rows),
        grid=(batch, NA_HEADS // 2),
        in_specs=[blk, blk, blk, pl.BlockSpec((None,) + bias.shape[1:], lambda b, hp: (hp, 0, 0, 0))],
        out_specs=blk,
        out_shape=jax.ShapeDtypeStruct((n, NA_HEADS * NA_HD), BF16),
        compiler_params=_params(("parallel", "parallel")),
        name="na_attn",
    )(q, k, v, bias)


def _na_bias_table(rpb):
    n_rel = 2 * NA_KW - 1
    period = 2 * GRID_W
    left = GRID_W - NA_KW
    p = jnp.pad(rpb.astype(F32), ((0, 0), (0, 0), (left, period - n_rel - left)))
    lead = p.shape[:-1]
    flat = jnp.broadcast_to(p[..., None, :], lead + (GRID_W, period)).reshape(lead + (GRID_W * period,))
    skew = flat[..., :GRID_W * (period - 1)].reshape(lead + (GRID_W, period - 1))
    by_col = skew[..., GRID_W - 1:2 * GRID_W - 1]
    cols = np.arange(GRID_W)
    col_start = np.clip(cols - NA_KW // 2, 0, GRID_W - NA_KW)
    kc = np.arange(GRID_W)
    in_win = (kc[None, :] >= col_start[:, None]) & (kc[None, :] < col_start[:, None] + NA_KW)
    by_col = jnp.where(jnp.asarray(in_win)[None, None], by_col * LOG2E, NEG)
    t = jnp.stack([by_col[:, NA_KH - 1 - dl:2 * NA_KH - 1 - dl] for dl in range(NA_KH)], axis=1)
    t = jnp.swapaxes(t, 2, 3).reshape(rpb.shape[0] // 2, 2, NA_KH, GRID_W, NA_KH * GRID_W)
    return jnp.swapaxes(t, 1, 2).reshape(rpb.shape[0] // 2, NA_KH, 2 * GRID_W, NA_KH * GRID_W)


def _attn_out_kernel(om_ref, on_ref, wm_ref, wn_ref, x_ref, gpost_ref, gnext_ref, xo_ref, ho_ref):
    m = _dot(om_ref[...], wm_ref[...]) + _dot(on_ref[...], wn_ref[...])
    xo = x_ref[...] + _rms(m, gpost_ref[...])
    xo_ref[...] = xo
    ho_ref[...] = _rms(xo, gnext_ref[...]).astype(BF16)


def _attn_out(o_mla, o_na, w_m, w_n, x, g_post, g_next, *, tile):
    n, d = x.shape
    tok = lambda w: pl.BlockSpec((tile, w), lambda i: (i, 0))
    return pl.pallas_call(
        _attn_out_kernel,
        grid=(n // tile,),
        in_specs=[tok(o_mla.shape[1]), tok(o_na.shape[1]), _const_spec(w_m.shape), _const_spec(w_n.shape),
                  tok(d), _const_spec(g_post.shape), _const_spec(g_next.shape)],
        out_specs=[tok(d), tok(d)],
        out_shape=[jax.ShapeDtypeStruct((n, d), F32), jax.ShapeDtypeStruct((n, d), BF16)],
        compiler_params=_params(("parallel",)),
        name="attn_out",
    )(o_mla, o_na, w_m, w_n, x, g_post, g_next)


def _halo_specs(tile, d, n):
    per = tile // HALO
    last = n // HALO - 1
    return [pl.BlockSpec((HALO, d), lambda i, *_: (jnp.maximum(i * per - 1, 0), 0)),
            pl.BlockSpec((tile, d), lambda i, *_: (i, 0)),
            pl.BlockSpec((HALO, d), lambda i, *_: (jnp.minimum((i + 1) * per, last), 0))]


def _fill_halo(hh_ref, hp_ref, hm_ref, hn_ref, pos0, tile, seq):
    zero = jnp.zeros(hp_ref.shape, hp_ref.dtype)
    hh_ref[0:HALO, :] = jnp.where(pos0 == 0, zero, hp_ref[...])
    hh_ref[HALO:HALO + tile, :] = hm_ref[...]
    hh_ref[HALO + tile:, :] = jnp.where(pos0 + tile == seq, zero, hn_ref[...])


def _dwconv(u, w, b, tile, left, chained_rolls=False):
    taps = w.shape[0]
    if chained_rolls:
        n = u.shape[0]
        shifted = {0: u}
        for d in range(1, taps - left):
            shifted[d] = pltpu.roll(shifted[d - 1], n - 1, axis=0)
        for d in range(1, left + 1):
            shifted[-d] = pltpu.roll(shifted[1 - d], 1, axis=0)
        y = b
        for kk in range(taps):
            y = y + w[kk:kk + 1, :] * shifted[kk - left][HALO:HALO + tile, :]
        return y
    y = b
    for kk in range(taps):
        s = HALO - left + kk
        y = y + w[kk:kk + 1, :] * u[s:s + tile, :]
    return y


def _slab_conv3(u, w, b, tile):
    s = tile // SUB
    slabs = [u[r * s:(r + 1) * s, :] for r in range(SUB)]
    row = lax.broadcasted_iota(jnp.int32, slabs[0].shape, 0)
    before = jnp.where(row == 0, u[tile:tile + 1, :], pltpu.roll(slabs[SUB - 1], 1, axis=0))
    after = jnp.where(row == s - 1, u[tile + 1:tile + 2, :], pltpu.roll(slabs[0], s - 1, axis=0))
    out = []
    for r in range(SUB):
        left = slabs[r - 1] if r > 0 else before
        right = slabs[r + 1] if r < SUB - 1 else after
        out.append(b + w[0:1, :] * left + w[1:2, :] * slabs[r] + w[2:3, :] * right)
    return jnp.concatenate(out, axis=0)


def _ffn_kernel(*refs, tile, seq, emit_next):
    if emit_next:
        (hp_ref, hm_ref, hn_ref, x_ref, wg_ref, wv_ref, cwg_ref, cwv_ref, cbg_ref, cbv_ref, wd_ref,
         gpost_ref, gnext_ref, xo_ref, ho_ref, hh_ref, acc_ref, nat_ref) = refs
    else:
        (hp_ref, hm_ref, hn_ref, x_ref, wg_ref, wv_ref, cwg_ref, cwv_ref, cbg_ref, cbv_ref, wd_ref,
         gpost_ref, xo_ref, hh_ref, acc_ref, nat_ref) = refs
    i = pl.program_id(0)
    c = pl.program_id(1)
    s = tile // SUB
    ng = nat_ref.shape[0]

    @pl.when(c == 0)
    def _():
        pos0 = lax.rem(i * tile, seq)
        for k in range(ng):
            nat_ref[k] = hm_ref[:, k * LANES:(k + 1) * LANES].astype(F32)
        for r in range(SUB):
            hh_ref[r * s:(r + 1) * s, :] = jnp.concatenate(
                [nat_ref[k, pl.ds(r, s, stride=SUB), :] for k in range(ng)], axis=1).astype(BF16)
        zero = jnp.zeros((1, hm_ref.shape[1]), F32)
        prev = jnp.where(pos0 == 0, zero, hp_ref[HALO - 1:HALO, :].astype(F32))
        nxt = jnp.where(pos0 + tile == seq, zero, hn_ref[0:1, :].astype(F32))
        tail = jnp.concatenate([prev, nxt, jnp.zeros((HALO - 2, hm_ref.shape[1]), F32)], axis=0)
        hh_ref[tile:, :] = tail.astype(BF16)
        acc_ref[...] = jnp.zeros(acc_ref.shape, F32)

    hh = hh_ref[...]
    g = _slab_conv3(_dot(hh, wg_ref[...]), cwg_ref[...], cbg_ref[...], tile)
    val = _slab_conv3(_dot(hh, wv_ref[...]), cwv_ref[...], cbv_ref[...], tile)
    a = (_gelu_tanh(g) * val).astype(BF16)
    acc_ref[...] += _dot(a, wd_ref[...])

    @pl.when(c == pl.num_programs(1) - 1)
    def _():
        f = _rms(acc_ref[...], gpost_ref[...])
        for r in range(SUB):
            for k in range(ng):
                nat_ref[k, pl.ds(r, s, stride=SUB), :] = f[r * s:(r + 1) * s, k * LANES:(k + 1) * LANES]
        xo = x_ref[...] + jnp.concatenate([nat_ref[k] for k in range(ng)], axis=1)
        xo_ref[...] = xo
        if emit_next:
            ho_ref[...] = _rms(xo, gnext_ref[...]).astype(BF16)


def _ffn(h, x, w_up, conv_w, conv_b, w_down, g_post, g_next, *, tile, fc, seq):
    n, d = x.shape
    d_ff = w_down.shape[0]
    nc = d_ff // fc
    kk = conv_w.shape[0]
    assert kk == 3 and d % LANES == 0 and tile % (SUB * SUB) == 0
    emit_next = g_next is not None
    tok = pl.BlockSpec((tile, d), lambda i, c: (i, 0))
    in_specs = _halo_specs(tile, d, n) + [
        tok,
        pl.BlockSpec((d, fc), lambda i, c: (0, c)),
        pl.BlockSpec((d, fc), lambda i, c: (0, nc + c)),
        pl.BlockSpec((kk, fc), lambda i, c: (0, c)),
        pl.BlockSpec((kk, fc), lambda i, c: (0, nc + c)),
        pl.BlockSpec((1, fc), lambda i, c: (0, c)),
        pl.BlockSpec((1, fc), lambda i, c: (0, nc + c)),
        pl.BlockSpec((fc, d), lambda i, c: (c, 0)),
        _const_spec(g_post.shape)]
    args = [h, h, h, x, w_up, w_up, conv_w, conv_w, conv_b, conv_b, w_down, g_post]
    out_specs = [tok]
    out_shape = [jax.ShapeDtypeStruct((n, d), F32)]
    if emit_next:
        in_specs.append(_const_spec(g_next.shape))
        args.append(g_next)
        out_specs.append(tok)
        out_shape.append(jax.ShapeDtypeStruct((n, d), BF16))
    res = pl.pallas_call(
        functools.partial(_ffn_kernel, tile=tile, seq=seq, emit_next=emit_next),
        grid=(n // tile, nc),
        in_specs=in_specs,
        out_specs=out_specs,
        out_shape=out_shape,
        scratch_shapes=[pltpu.VMEM((tile + HALO, d), BF16), pltpu.VMEM((tile, d), F32),
                        pltpu.VMEM((d // LANES, tile, LANES), F32)],
        compiler_params=_params(("parallel", "arbitrary")),
        name="conv_ffn",
    )(*args)
    return (res[0], res[1]) if emit_next else (res[0], None)


def _lru_coeffs(xc, xc_bf, gw_ref, gab_ref, gxb_ref, lam_ref):
    nb, bw, _ = gw_ref.shape
    neg = -lam_ref[...]
    sp = jnp.maximum(neg, 0.0) + jnp.log1p(jnp.exp(-jnp.abs(neg)))
    half_c = (-0.5 * LRU_C) * sp
    half_gab = 0.5 * gab_ref[...]
    half_gxb = 0.5 * gxb_ref[...]
    a_parts, b_parts = [], []
    for blk in range(nb):
        sl = slice(blk * bw, (blk + 1) * bw)
        rg = _dot(xc_bf[:, sl], gw_ref[blk])
        hc = half_c[:, sl]
        log_a = hc * jnp.tanh(rg[:, :bw] + half_gab[:, sl]) + hc
        hx = 0.5 * xc[:, sl]
        gated_x = hx * jnp.tanh(rg[:, bw:] + half_gxb[:, sl]) + hx
        t = jnp.tanh(log_a)
        one_minus_a2 = (-2.0 * t) / (1.0 - t)
        root = jnp.where(one_minus_a2 > 0.0, one_minus_a2 * lax.rsqrt(one_minus_a2), 0.0)
        a_parts.append(jnp.exp2(log_a * LOG2E))
        b_parts.append(root * gated_x)
    return jnp.concatenate(a_parts, axis=1), jnp.concatenate(b_parts, axis=1)


def _scan_tile(a_ref, b_ref, carry_ref, *, reverse):
    t, ch = a_ref.shape
    n_chunks = t // SUB
    row = lax.broadcasted_iota(jnp.int32, (SUB, ch), 0)

    def body(j, carry):
        jj = (n_chunks - 1 - j) if reverse else j
        r0 = pl.multiple_of(jj * SUB, SUB)
        a = a_ref[pl.ds(r0, SUB), :]
        b = b_ref[pl.ds(r0, SUB), :]
        for dist in (1, 2, 4):
            shift = (SUB - dist) if reverse else dist
            a_sh = pltpu.roll(a, shift, axis=0)
            b_sh = pltpu.roll(b, shift, axis=0)
            valid = (row < SUB - dist) if reverse else (row >= dist)
            b = jnp.where(valid, a * b_sh + b, b)
            a = jnp.where(valid, a * a_sh, a)
        hcur = a * carry + b
        b_ref[pl.ds(r0, SUB), :] = hcur
        edge = hcur[0:1, :] if reverse else hcur[SUB - 1:SUB, :]
        return jnp.broadcast_to(edge, (SUB, ch))

    carry_ref[...] = lax.fori_loop(0, n_chunks, body, carry_ref[...], unroll=2)


def _rec_fwd_kernel(hp_ref, hm_ref, hn_ref, wg_ref, wx_ref, cw_ref, cb_ref, gw_ref, gab_ref, gxb_ref,
                    lam_ref, gate_ref, xc_ref, hf_ref, hh_ref, a_ref, b_ref, carry_ref, *, tile, seq):
    pos0 = lax.rem(pl.program_id(0) * tile, seq)
    _fill_halo(hh_ref, hp_ref, hm_ref, hn_ref, pos0, tile, seq)

    @pl.when(pos0 == 0)
    def _():
        carry_ref[...] = jnp.zeros(carry_ref.shape, F32)

    gate_ref[...] = _dot(hm_ref[...], wg_ref[...]).astype(BF16)
    left = (cw_ref.shape[0] - 1) // 2
    xc = _dwconv(_dot(hh_ref[...], wx_ref[...]), cw_ref[...], cb_ref[...], tile, left, chained_rolls=True)
    xc_bf = xc.astype(BF16)
    xc_ref[...] = xc_bf
    a, b = _lru_coeffs(xc, xc_bf, gw_ref, gab_ref, gxb_ref, lam_ref)
    a_ref[...] = a
    b_ref[...] = b
    _scan_tile(a_ref, b_ref, carry_ref, reverse=False)
    hf_ref[...] = b_ref[...].astype(BF16)


def _rec_fwd(h, w_g, w_x, conv_w, conv_b, gw, gab, gxb, lam, *, tile, seq):
    n, d = h.shape
    d_rnn = w_x.shape[1]
    tok = pl.BlockSpec((tile, d_rnn), lambda i: (i, 0))
    in_specs = _halo_specs(tile, d, n) + [_const_spec(a.shape) for a in
                                          (w_g, w_x, conv_w, conv_b, gw, gab, gxb, lam)]
    out = jax.ShapeDtypeStruct((n, d_rnn), BF16)
    return pl.pallas_call(
        functools.partial(_rec_fwd_kernel, tile=tile, seq=seq),
        grid=(n // tile,),
        in_specs=in_specs,
        out_specs=[tok, tok, tok],
        out_shape=[out, out, out],
        scratch_shapes=[pltpu.VMEM((tile + 2 * HALO, d), BF16), pltpu.VMEM((tile, d_rnn), F32),
                        pltpu.VMEM((tile, d_rnn), F32), pltpu.VMEM((SUB, d_rnn), F32)],
        compiler_params=_params(("arbitrary",)),
        name="rglru_fwd",
    )(h, h, h, w_g, w_x, conv_w, conv_b, gw, gab, gxb, lam)


def _rec_bwd_kernel(xc_ref, gate_ref, hf_ref, x_ref, gw_ref, gab_ref, gxb_ref, lam_ref, wo_ref,
                    gpost_ref, gnext_ref, xo_ref, ho_ref, a_ref, b_ref, carry_ref, *, tile, seq):
    i = pl.num_programs(0) - 1 - pl.program_id(0)
    pos0 = lax.rem(i * tile, seq)

    @pl.when(pos0 + tile == seq)
    def _():
        carry_ref[...] = jnp.zeros(carry_ref.shape, F32)

    xc_bf = xc_ref[...]
    a, b = _lru_coeffs(xc_bf.astype(F32), xc_bf, gw_ref, gab_ref, gxb_ref, lam_ref)
    a_ref[...] = a
    b_ref[...] = b
    _scan_tile(a_ref, b_ref, carry_ref, reverse=True)
    y = (hf_ref[...].astype(F32) + b_ref[...]) * _gelu_tanh(gate_ref[...].astype(F32))
    m = _dot(y.astype(BF16), wo_ref[...])
    xo = x_ref[...] + _rms(m, gpost_ref[...])
    xo_ref[...] = xo
    ho_ref[...] = _rms(xo, gnext_ref[...]).astype(BF16)


def _rec_bwd(xc, gate, hf, x, gw, gab, gxb, lam, w_out, g_post, g_next, *, tile, seq):
    n, d = x.shape
    d_rnn = xc.shape[1]
    nt = n // tile
    rtok = lambda w: pl.BlockSpec((tile, w), lambda i: (nt - 1 - i, 0))
    in_specs = [rtok(d_rnn), rtok(d_rnn), rtok(d_rnn), rtok(d)] + [
        _const_spec(a.shape) for a in (gw, gab, gxb, lam, w_out, g_post, g_next)]
    return pl.pallas_call(
        functools.partial(_rec_bwd_kernel, tile=tile, seq=seq),
        grid=(nt,),
        in_specs=in_specs,
        out_specs=[rtok(d), rtok(d)],
        out_shape=[jax.ShapeDtypeStruct((n, d), F32), jax.ShapeDtypeStruct((n, d), BF16)],
        scratch_shapes=[pltpu.VMEM((tile, d_rnn), F32), pltpu.VMEM((tile, d_rnn), F32),
                        pltpu.VMEM((SUB, d_rnn), F32)],
        compiler_params=_params(("arbitrary",)),
        name="rglru_bwd",
    )(xc, gate, hf, x, gw, gab, gxb, lam, w_out, g_post, g_next)


def _prenorm_kernel(x_ref, g_ref, h_ref):
    h_ref[...] = _rms(x_ref[...], g_ref[...]).astype(BF16)


def _prenorm(x, g, *, tile):
    n, d = x.shape
    tok = pl.BlockSpec((tile, d), lambda i: (i, 0))
    return pl.pallas_call(
        _prenorm_kernel, grid=(n // tile,), in_specs=[tok, _const_spec(g.shape)], out_specs=tok,
        out_shape=jax.ShapeDtypeStruct((n, d), BF16), compiler_params=_params(("parallel",)),
        name="prenorm",
    )(x, g)


def _row(v):
    return v.reshape(1, -1).astype(F32)


def _rope_tables(s_max):
    pos = jnp.arange(s_max, dtype=F32)
    inv = ROPE_BASE ** (-jnp.arange(0, MLA_ROPE, 2, dtype=F32) / MLA_ROPE)
    ang = pos[:, None] * inv[None, :]
    cos, sin = jnp.cos(ang), jnp.sin(ang)
    pad = LANES - MLA_NOPE - MLA_ROPE
    cos_t = jnp.concatenate([jnp.ones((s_max, MLA_NOPE), F32), cos, cos, jnp.zeros((s_max, pad), F32)], axis=1)
    sin_t = jnp.concatenate([jnp.zeros((s_max, MLA_NOPE), F32), -sin, sin, jnp.zeros((s_max, pad), F32)], axis=1)
    return cos_t, sin_t


def _attn_weights(w_in, w_uq, w_ukv):
    d = w_in.shape[0]
    q_lora = w_uq.shape[0]
    kv_lora = w_ukv.shape[0]
    o1 = q_lora + kv_lora
    o2 = o1 + MLA_ROPE
    pad = LANES - MLA_NOPE - MLA_ROPE
    kpe = w_in[:, o1:o2]
    zl = jnp.zeros((d, MLA_NOPE), w_in.dtype)
    zr = jnp.zeros((d, pad), w_in.dtype)
    w_all = jnp.concatenate([w_in[:, :o1], zl, kpe, zr, w_in[:, o2:]], axis=1)

    wq = w_uq.reshape(q_lora, MLA_HEADS, MLA_NOPE + MLA_ROPE)
    zq = jnp.zeros((q_lora, MLA_HEADS, pad), w_uq.dtype)
    wq_p = jnp.concatenate([wq, zq], axis=-1).reshape(q_lora, MLA_HEADS * LANES)

    wkv = w_ukv.reshape(kv_lora, MLA_HEADS, MLA_NOPE + MLA_V)
    zk = jnp.zeros((kv_lora, MLA_HEADS, LANES - MLA_NOPE), w_ukv.dtype)
    wk_p = jnp.concatenate([wkv[..., :MLA_NOPE], zk], axis=-1).reshape(kv_lora, MLA_HEADS * LANES)
    wv = wkv[..., MLA_NOPE:].reshape(kv_lora, MLA_HEADS * MLA_V)
    return tuple(a.astype(BF16) for a in (w_all, wq_p, wk_p, wv))


TOKEN_TILE = 512
FFN_TILE = 1024
FFN_CHUNK = 1024
REC_TILE = 512
MLA_SCORE_ELEMS = 2 ** 21


def kernel(x_prompt, x_sample, norm_mix_pre, norm_mix_post, norm_ffn_pre, norm_ffn_post, w_in_attn, q_norm, w_uq, kv_norm, w_ukv, na_rpb, w_out_attn, w_in_rec, conv_w_rec, conv_b_rec, gate_a_w, gate_a_b, gate_x_w, gate_x_b, lru_lambda, w_out_rec, w_ffn_up, conv_w_ffn, conv_b_ffn, w_ffn_down):
    depth = norm_mix_pre.shape[0]
    d_rnn = w_out_rec.shape[1]
    n_mla = MLA_HEADS * MLA_V
    cos_t, sin_t = _rope_tables(max(x_prompt.shape[1], x_sample.shape[1]))

    layers = []
    for li in range(depth):
        j = li // 2
        lw = dict(ffn_up=w_ffn_up[li].astype(BF16), ffn_down=w_ffn_down[li].astype(BF16))
        if li % 2 == 0:
            lw["attn"] = _attn_weights(w_in_attn[j], w_uq[j], w_ukv[j])
            lw["bias"] = _na_bias_table(na_rpb[j])
            w_o = w_out_attn[j].astype(BF16)
            lw["out"] = (w_o[:n_mla], w_o[n_mla:])
        else:
            w_in = w_in_rec[j].astype(BF16)
            lw["rec_in"] = (w_in[:, :d_rnn], w_in[:, d_rnn:])
            lw["gates"] = [(0.5 * jnp.concatenate([gate_a_w[j, e], gate_x_w[j, e]], axis=-1)).astype(BF16)
                           for e in range(2)]
            lw["rec_out"] = w_out_rec[j].astype(BF16)
        layers.append(lw)

    def trunk(x3):
        batch, seq, d = x3.shape
        x = x3.reshape(batch * seq, d)
        tile = min(TOKEN_TILE, seq)
        h = None
        for li, lw in enumerate(layers):
            j = li // 2
            if li % 2 == 0:
                w_all, wq_p, wk_p, wv = lw["attn"]
                q, k, v, naq, nak, nav = _attn_in(
                    x, _row(norm_mix_pre[li]), w_all, _row(q_norm[j]), wq_p, _row(kv_norm[j]), wk_p, wv,
                    cos_t, sin_t, tile=tile, seq=seq)
                o_mla = _mla(q, k, v, batch=batch, seq=seq, tq=min(seq, MLA_SCORE_ELEMS // seq))
                o_na = _na(naq, nak, nav, lw["bias"], batch=batch, seq=seq)
                x, h = _attn_out(o_mla, o_na, lw["out"][0], lw["out"][1], x, _row(norm_mix_post[li]),
                                 _row(norm_ffn_pre[li]), tile=tile)
            else:
                if h is None:
                    h = _prenorm(x, _row(norm_mix_pre[li]), tile=tile)
                rec_tile = min(REC_TILE, seq)
                gate, xc, hf = _rec_fwd(h, lw["rec_in"][0], lw["rec_in"][1], conv_w_rec[j], _row(conv_b_rec[j]),
                                        lw["gates"][0], _row(gate_a_b[j, 0]), _row(gate_x_b[j, 0]),
                                        _row(lru_lambda[j, 0]), tile=rec_tile, seq=seq)
                x, h = _rec_bwd(xc, gate, hf, x, lw["gates"][1], _row(gate_a_b[j, 1]), _row(gate_x_b[j, 1]),
                                _row(lru_lambda[j, 1]), lw["rec_out"], _row(norm_mix_post[li]),
                                _row(norm_ffn_pre[li]), tile=rec_tile, seq=seq)
            g_next = _row(norm_mix_pre[li + 1]) if li + 1 < depth else None
            x, h = _ffn(h, x, lw["ffn_up"], conv_w_ffn[li], _row(conv_b_ffn[li]), lw["ffn_down"],
                        _row(norm_ffn_post[li]), g_next, tile=min(FFN_TILE, seq),
                        fc=min(FFN_CHUNK, lw["ffn_down"].shape[0]), seq=seq)
        return x.reshape(batch, seq, d)

    return trunk(x_prompt), trunk(x_sample)
```

```python
import functools
import math

import numpy as np
import jax
import jax.numpy as jnp
from jax import lax
from jax.experimental import pallas as pl
from jax.experimental.pallas import tpu as pltpu

F32 = jnp.float32
BF16 = jnp.bfloat16

EPS = 1e-6
GRID_W = 64
MLA_HEADS = 8
MLA_NOPE = 64
MLA_ROPE = 32
MLA_V = 64
ROPE_BASE = 10000.0
NA_HEADS = 8
NA_HD = 64
NA_KH = 8
NA_KW = 16
LRU_C = 8.0

LANES = 128
SUB = 8
HALO = 16
NEG = -1e30
VMEM_LIMIT = 56 * 1024 * 1024
LOG2E = math.log2(math.e)


def _rms(x, g):
    return x * lax.rsqrt(jnp.mean(x * x, axis=-1, keepdims=True) + EPS) * g


def _gelu_tanh(x):
    c1 = math.sqrt(2.0 / math.pi)
    k1 = -2.0 * c1 * LOG2E
    k2 = k1 * 0.044715
    e = jnp.exp2(x * ((x * x) * k2 + k1))
    return x / (1.0 + e)


def _dot(a, b):
    return jnp.dot(a, b, preferred_element_type=F32)


def _dot_nt(a, b):
    return lax.dot_general(a, b, (((1,), (1,)), ((), ())), preferred_element_type=F32)


def _params(sem):
    return pltpu.CompilerParams(dimension_semantics=sem, vmem_limit_bytes=VMEM_LIMIT)


def _const_spec(shape):
    nd = len(shape)
    return pl.BlockSpec(shape, lambda *_: (0,) * nd)


def _rope(x, cos, sin):
    half = MLA_ROPE // 2
    lane = lax.broadcasted_iota(jnp.int32, x.shape, 1)
    partner = jnp.where(lane < MLA_NOPE + half, pltpu.roll(x, LANES - half, axis=1), pltpu.roll(x, half, axis=1))
    return x * cos + partner * sin


def _attn_in_kernel(x_ref, g_ref, win_ref, qn_ref, wq_ref, kvn_ref, wk_ref, wv_ref,
                    cos_ref, sin_ref,
                    q_ref, k_ref, v_ref, naq_ref, nak_ref, nav_ref, *, q_lora, kv_lora, q_scale, na_scale):
    h = _rms(x_ref[...], g_ref[...]).astype(BF16)
    z = _dot(h, win_ref[...])
    o0 = q_lora
    o1 = o0 + kv_lora
    o3 = o1 + LANES
    na_w = NA_HEADS * NA_HD
    cos = cos_ref[...]
    sin = sin_ref[...]

    nq = _rms(z[:, :o0], qn_ref[...]).astype(BF16)
    q = _dot(nq, wq_ref[...])
    for hd in range(MLA_HEADS):
        sl = slice(hd * LANES, (hd + 1) * LANES)
        q_ref[:, sl] = (_rope(q[:, sl], cos, sin) * q_scale).astype(BF16)

    nkv = _rms(z[:, o0:o1], kvn_ref[...]).astype(BF16)
    k = _dot(nkv, wk_ref[...])
    rot = _rope(z[:, o1:o3], cos, sin)
    for hd in range(MLA_HEADS):
        sl = slice(hd * LANES, (hd + 1) * LANES)
        k_ref[:, sl] = (k[:, sl] + rot).astype(BF16)
    v_ref[...] = _dot(nkv, wv_ref[...]).astype(BF16)

    naq_ref[...] = (z[:, o3:o3 + na_w] * na_scale).astype(BF16)
    nak_ref[...] = z[:, o3 + na_w:o3 + 2 * na_w].astype(BF16)
    nav_ref[...] = z[:, o3 + 2 * na_w:o3 + 3 * na_w].astype(BF16)


def _attn_in(x, g, w_all, qn, wq, kvn, wk, wv, cos_t, sin_t, *, tile, seq):
    n, d = x.shape
    q_lora = qn.shape[1]
    kv_lora = kvn.shape[1]
    hw = MLA_HEADS * LANES
    vw = MLA_HEADS * MLA_V
    na_w = NA_HEADS * NA_HD
    per_seq = seq // tile

    tok = lambda w: pl.BlockSpec((tile, w), lambda i: (i, 0))
    rope = pl.BlockSpec((tile, LANES), lambda i: (lax.rem(i, per_seq), 0))
    kern = functools.partial(_attn_in_kernel, q_lora=q_lora, kv_lora=kv_lora,
                             q_scale=float((MLA_NOPE + MLA_ROPE) ** -0.5 * LOG2E),
                             na_scale=float(NA_HD ** -0.5 * LOG2E))
    return pl.pallas_call(
        kern,
        grid=(n // tile,),
        in_specs=[tok(d), _const_spec(g.shape), _const_spec(w_all.shape), _const_spec(qn.shape),
                  _const_spec(wq.shape), _const_spec(kvn.shape),
                  _const_spec(wk.shape), _const_spec(wv.shape), rope, rope],
        out_specs=[tok(hw), tok(hw), tok(vw), tok(na_w), tok(na_w), tok(na_w)],
        out_shape=[jax.ShapeDtypeStruct((n, hw), BF16), jax.ShapeDtypeStruct((n, hw), BF16),
                   jax.ShapeDtypeStruct((n, vw), BF16),
                   jax.ShapeDtypeStruct((n, na_w), BF16), jax.ShapeDtypeStruct((n, na_w), BF16),
                   jax.ShapeDtypeStruct((n, na_w), BF16)],
        compiler_params=_params(("parallel",)),
        name="attn_in",
    )(x, g, w_all, qn, wq, kvn, wk, wv, cos_t, sin_t)


def _mla_kernel(q_ref, k_ref, v_ref, o_ref):
    tq = q_ref.shape[0]
    seq = k_ref.shape[0]
    heads = [slice(hd * LANES, (hd + 1) * LANES) for hd in range(2)]
    scores = [_dot_nt(q_ref[:, hsl], k_ref[:, hsl]) for hsl in heads]
    probs = [jnp.exp2((s - jnp.max(s, axis=-1, keepdims=True)).astype(BF16)) for s in scores]
    v_ext = jnp.concatenate([v_ref[...], jnp.ones((seq, LANES), BF16)], axis=1)
    accs = [_dot(p, v_ext) for p in probs]
    outs = [acc[:, :LANES] / acc[:, LANES:LANES + 1] for acc in accs]
    lane = lax.broadcasted_iota(jnp.int32, (tq, LANES), 1)
    o_ref[...] = jnp.where(lane < MLA_V, outs[0], outs[1]).astype(BF16)


def _mla(q, k, v, *, batch, seq, tq):
    n = q.shape[0]
    nq = seq // tq
    return pl.pallas_call(
        _mla_kernel,
        grid=(batch, MLA_HEADS // 2, nq),
        in_specs=[pl.BlockSpec((tq, 2 * LANES), lambda b, hp, i: (b * nq + i, hp)),
                  pl.BlockSpec((seq, 2 * LANES), lambda b, hp, i: (b, hp)),
                  pl.BlockSpec((seq, LANES), lambda b, hp, i: (b, hp))],
        out_specs=pl.BlockSpec((tq, LANES), lambda b, hp, i: (b * nq + i, hp)),
        out_shape=jax.ShapeDtypeStruct((n, MLA_HEADS * MLA_V), BF16),
        compiler_params=_params(("parallel", "parallel", "arbitrary")),
        name="mla_attn",
    )(q, k, v)


NA_GROUP = 32


def _na_kernel(q_ref, k_ref, v_ref, bias_ref, o_ref, *, rows):
    win = NA_KH * GRID_W
    lane_q = lax.broadcasted_iota(jnp.int32, (GRID_W, LANES), 1)
    ones = jnp.ones((win, LANES), BF16)
    group = math.gcd(NA_GROUP, rows)

    def body(g, _):
        q0s, k0s, scores = [], [], []
        for j in range(group):
            r = g * group + j
            rs = jnp.clip(r - NA_KH // 2, 0, rows - NA_KH)
            q0 = pl.multiple_of(r * GRID_W, GRID_W)
            k0 = pl.multiple_of(rs * GRID_W, GRID_W)
            qr = q_ref[pl.ds(q0, GRID_W), :]
            zero = jnp.zeros_like(qr)
            q2 = jnp.concatenate([jnp.where(lane_q < NA_HD, qr, zero), jnp.where(lane_q >= NA_HD, qr, zero)],
                                 axis=0)
            scores.append(_dot_nt(q2, k_ref[pl.ds(k0, win), :]) + bias_ref[r - rs])
            q0s.append(q0)
            k0s.append(k0)
        probs = [jnp.exp2((s - jnp.max(s, axis=-1, keepdims=True)).astype(BF16)) for s in scores]
        accs = [_dot(p, jnp.concatenate([v_ref[pl.ds(k0, win), :], ones], axis=1)) for p, k0 in zip(probs, k0s)]
        for q0, acc in zip(q0s, accs):
            o2 = acc[:, :LANES] / acc[:, LANES:LANES + 1]
            o_ref[pl.ds(q0, GRID_W), :] = jnp.where(lane_q < NA_HD, o2[:GRID_W], o2[GRID_W:]).astype(BF16)
        return 0

    lax.fori_loop(0, rows // group, body, 0)


def _na(q, k, v, bias, *, batch, seq):
    n = q.shape[0]
    rows = seq // GRID_W
    blk = pl.BlockSpec((seq, LANES), lambda b, hp: (b, hp))
    return pl.pallas_call(
        functools.partial(_na_kernel, rows=rows),
        grid=(batch, NA_HEADS // 2),
        in_specs=[blk, blk, blk, pl.BlockSpec((None,) + bias.shape[1:], lambda b, hp: (hp, 0, 0, 0))],
        out_specs=blk,
        out_shape=jax.ShapeDtypeStruct((n, NA_HEADS * NA_HD), BF16),
        compiler_params=_params(("parallel", "parallel")),
        name="na_attn",
    )(q, k, v, bias)


def _na_bias_table(rpb):
    n_rel = 2 * NA_KW - 1
    period = 2 * GRID_W
    left = GRID_W - NA_KW
    p = jnp.pad(rpb.astype(F32), ((0, 0), (0, 0), (left, period - n_rel - left)))
    lead = p.shape[:-1]
    flat = jnp.broadcast_to(p[..., None, :], lead + (GRID_W, period)).reshape(lead + (GRID_W * period,))
    skew = flat[..., :GRID_W * (period - 1)].reshape(lead + (GRID_W, period - 1))
    by_col = skew[..., GRID_W - 1:2 * GRID_W - 1]
    cols = np.arange(GRID_W)
    col_start = np.clip(cols - NA_KW // 2, 0, GRID_W - NA_KW)
    kc = np.arange(GRID_W)
    in_win = (kc[None, :] >= col_start[:, None]) & (kc[None, :] < col_start[:, None] + NA_KW)
    by_col = jnp.where(jnp.asarray(in_win)[None, None], by_col * LOG2E, NEG)
    t = jnp.stack([by_col[:, NA_KH - 1 - dl:2 * NA_KH - 1 - dl] for dl in range(NA_KH)], axis=1)
    t = jnp.swapaxes(t, 2, 3).reshape(rpb.shape[0] // 2, 2, NA_KH, GRID_W, NA_KH * GRID_W)
    return jnp.swapaxes(t, 1, 2).reshape(rpb.shape[0] // 2, NA_KH, 2 * GRID_W, NA_KH * GRID_W)


def _attn_out_kernel(om_ref, on_ref, wm_ref, wn_ref, x_ref, gpost_ref, gnext_ref, xo_ref, ho_ref):
    m = _dot(om_ref[...], wm_ref[...]) + _dot(on_ref[...], wn_ref[...])
    xo = x_ref[...] + _rms(m, gpost_ref[...])
    xo_ref[...] = xo
    ho_ref[...] = _rms(xo, gnext_ref[...]).astype(BF16)


def _attn_out(o_mla, o_na, w_m, w_n, x, g_post, g_next, *, tile):
    n, d = x.shape
    tok = lambda w: pl.BlockSpec((tile, w), lambda i: (i, 0))
    return pl.pallas_call(
        _attn_out_kernel,
        grid=(n // tile,),
        in_specs=[tok(o_mla.shape[1]), tok(o_na.shape[1]), _const_spec(w_m.shape), _const_spec(w_n.shape),
                  tok(d), _const_spec(g_post.shape), _const_spec(g_next.shape)],
        out_specs=[tok(d), tok(d)],
        out_shape=[jax.ShapeDtypeStruct((n, d), F32), jax.ShapeDtypeStruct((n, d), BF16)],
        compiler_params=_params(("parallel",)),
        name="attn_out",
    )(o_mla, o_na, w_m, w_n, x, g_post, g_next)


def _halo_specs(tile, d, n):
    per = tile // HALO
    last = n // HALO - 1
    return [pl.BlockSpec((HALO, d), lambda i, *_: (jnp.maximum(i * per - 1, 0), 0)),
            pl.BlockSpec((tile, d), lambda i, *_: (i, 0)),
            pl.BlockSpec((HALO, d), lambda i, *_: (jnp.minimum((i + 1) * per, last), 0))]


def _fill_halo(hh_ref, hp_ref, hm_ref, hn_ref, pos0, tile, seq):
    zero = jnp.zeros(hp_ref.shape, hp_ref.dtype)
    hh_ref[0:HALO, :] = jnp.where(pos0 == 0, zero, hp_ref[...])
    hh_ref[HALO:HALO + tile, :] = hm_ref[...]
    hh_ref[HALO + tile:, :] = jnp.where(pos0 + tile == seq, zero, hn_ref[...])


def _dwconv(u, w, b, tile, left, chained_rolls=False):
    taps = w.shape[0]
    if chained_rolls:
        n = u.shape[0]
        shifted = {0: u}
        for d in range(1, taps - left):
            shifted[d] = pltpu.roll(shifted[d - 1], n - 1, axis=0)
        for d in range(1, left + 1):
            shifted[-d] = pltpu.roll(shifted[1 - d], 1, axis=0)
        y = b
        for kk in range(taps):
            y = y + w[kk:kk + 1, :] * shifted[kk - left][HALO:HALO + tile, :]
        return y
    y = b
    for kk in range(taps):
        s = HALO - left + kk
        y = y + w[kk:kk + 1, :] * u[s:s + tile, :]
    return y


def _slab_conv3(u, w, b, tile):
    s = tile // SUB
    slabs = [u[r * s:(r + 1) * s, :] for r in range(SUB)]
    row = lax.broadcasted_iota(jnp.int32, slabs[0].shape, 0)
    before = jnp.where(row == 0, u[tile:tile + 1, :], pltpu.roll(slabs[SUB - 1], 1, axis=0))
    after = jnp.where(row == s - 1, u[tile + 1:tile + 2, :], pltpu.roll(slabs[0], s - 1, axis=0))
    out = []
    for r in range(SUB):
        left = slabs[r - 1] if r > 0 else before
        right = slabs[r + 1] if r < SUB - 1 else after
        out.append(b + w[0:1, :] * left + w[1:2, :] * slabs[r] + w[2:3, :] * right)
    return jnp.concatenate(out, axis=0)


def _ffn_kernel(*refs, tile, seq, emit_next):
    if emit_next:
        (hp_ref, hm_ref, hn_ref, x_ref, wg_ref, wv_ref, cwg_ref, cwv_ref, cbg_ref, cbv_ref, wd_ref,
         gpost_ref, gnext_ref, xo_ref, ho_ref, hh_ref, acc_ref, nat_ref) = refs
    else:
        (hp_ref, hm_ref, hn_ref, x_ref, wg_ref, wv_ref, cwg_ref, cwv_ref, cbg_ref, cbv_ref, wd_ref,
         gpost_ref, xo_ref, hh_ref, acc_ref, nat_ref) = refs
    i = pl.program_id(0)
    c = pl.program_id(1)
    s = tile // SUB
    ng = nat_ref.shape[0]

    @pl.when(c == 0)
    def _():
        pos0 = lax.rem(i * tile, seq)
        for k in range(ng):
            nat_ref[k] = hm_ref[:, k * LANES:(k + 1) * LANES].astype(F32)
        for r in range(SUB):
            hh_ref[r * s:(r + 1) * s, :] = jnp.concatenate(
                [nat_ref[k, pl.ds(r, s, stride=SUB), :] for k in range(ng)], axis=1).astype(BF16)
        zero = jnp.zeros((1, hm_ref.shape[1]), F32)
        prev = jnp.where(pos0 == 0, zero, hp_ref[HALO - 1:HALO, :].astype(F32))
        nxt = jnp.where(pos0 + tile == seq, zero, hn_ref[0:1, :].astype(F32))
        tail = jnp.concatenate([prev, nxt, jnp.zeros((HALO - 2, hm_ref.shape[1]), F32)], axis=0)
        hh_ref[tile:, :] = tail.astype(BF16)
        acc_ref[...] = jnp.zeros(acc_ref.shape, F32)

    hh = hh_ref[...]
    g = _slab_conv3(_dot(hh, wg_ref[...]), cwg_ref[...], cbg_ref[...], tile)
    val = _slab_conv3(_dot(hh, wv_ref[...]), cwv_ref[...], cbv_ref[...], tile)
    a = (_gelu_tanh(g) * val).astype(BF16)
    acc_ref[...] += _dot(a, wd_ref[...])

    @pl.when(c == pl.num_programs(1) - 1)
    def _():
        f = _rms(acc_ref[...], gpost_ref[...])
        for r in range(SUB):
            for k in range(ng):
                nat_ref[k, pl.ds(r, s, stride=SUB), :] = f[r * s:(r + 1) * s, k * LANES:(k + 1) * LANES]
        xo = x_ref[...] + jnp.concatenate([nat_ref[k] for k in range(ng)], axis=1)
        xo_ref[...] = xo
        if emit_next:
            ho_ref[...] = _rms(xo, gnext_ref[...]).astype(BF16)


def _ffn(h, x, w_up, conv_w, conv_b, w_down, g_post, g_next, *, tile, fc, seq):
    n, d = x.shape
    d_ff = w_down.shape[0]
    nc = d_ff // fc
    kk = conv_w.shape[0]
    assert kk == 3 and d % LANES == 0 and tile % (SUB * SUB) == 0
    emit_next = g_next is not None
    tok = pl.BlockSpec((tile, d), lambda i, c: (i, 0))
    in_specs = _halo_specs(tile, d, n) + [
        tok,
        pl.BlockSpec((d, fc), lambda i, c: (0, c)),
        pl.BlockSpec((d, fc), lambda i, c: (0, nc + c)),
        pl.BlockSpec((kk, fc), lambda i, c: (0, c)),
        pl.BlockSpec((kk, fc), lambda i, c: (0, nc + c)),
        pl.BlockSpec((1, fc), lambda i, c: (0, c)),
        pl.BlockSpec((1, fc), lambda i, c: (0, nc + c)),
        pl.BlockSpec((fc, d), lambda i, c: (c, 0)),
        _const_spec(g_post.shape)]
    args = [h, h, h, x, w_up, w_up, conv_w, conv_w, conv_b, conv_b, w_down, g_post]
    out_specs = [tok]
    out_shape = [jax.ShapeDtypeStruct((n, d), F32)]
    if emit_next:
        in_specs.append(_const_spec(g_next.shape))
        args.append(g_next)
        out_specs.append(tok)
        out_shape.append(jax.ShapeDtypeStruct((n, d), BF16))
    res = pl.pallas_call(
        functools.partial(_ffn_kernel, tile=tile, seq=seq, emit_next=emit_next),
        grid=(n // tile, nc),
        in_specs=in_specs,
        out_specs=out_specs,
        out_shape=out_shape,
        scratch_shapes=[pltpu.VMEM((tile + HALO, d), BF16), pltpu.VMEM((tile, d), F32),
                        pltpu.VMEM((d // LANES, tile, LANES), F32)],
        compiler_params=_params(("parallel", "arbitrary")),
        name="conv_ffn",
    )(*args)
    return (res[0], res[1]) if emit_next else (res[0], None)


def _lru_coeffs(xc, xc_bf, gw_ref, gab_ref, gxb_ref, lam_ref):
    nb, bw, _ = gw_ref.shape
    neg = -lam_ref[...]
    sp = jnp.maximum(neg, 0.0) + jnp.log1p(jnp.exp(-jnp.abs(neg)))
    half_c = (-0.5 * LRU_C) * sp
    half_gab = 0.5 * gab_ref[...]
    half_gxb = 0.5 * gxb_ref[...]
    a_parts, b_parts = [], []
    for blk in range(nb):
        sl = slice(blk * bw, (blk + 1) * bw)
        rg = _dot(xc_bf[:, sl], gw_ref[blk])
        hc = half_c[:, sl]
        log_a = hc * jnp.tanh(rg[:, :bw] + half_gab[:, sl]) + hc
        hx = 0.5 * xc[:, sl]
        gated_x = hx * jnp.tanh(rg[:, bw:] + half_gxb[:, sl]) + hx
        t = jnp.tanh(log_a)
        one_minus_a2 = (-2.0 * t) / (1.0 - t)
        root = jnp.where(one_minus_a2 > 0.0, one_minus_a2 * lax.rsqrt(one_minus_a2), 0.0)
        a_parts.append(jnp.exp2(log_a * LOG2E))
        b_parts.append(root * gated_x)
    return jnp.concatenate(a_parts, axis=1), jnp.concatenate(b_parts, axis=1)


def _scan_tile(a_ref, b_ref, carry_ref, *, reverse):
    t, ch = a_ref.shape
    n_chunks = t // SUB
    row = lax.broadcasted_iota(jnp.int32, (SUB, ch), 0)

    def body(j, carry):
        jj = (n_chunks - 1 - j) if reverse else j
        r0 = pl.multiple_of(jj * SUB, SUB)
        a = a_ref[pl.ds(r0, SUB), :]
        b = b_ref[pl.ds(r0, SUB), :]
        for dist in (1, 2, 4):
            shift = (SUB - dist) if reverse else dist
            a_sh = pltpu.roll(a, shift, axis=0)
            b_sh = pltpu.roll(b, shift, axis=0)
            valid = (row < SUB - dist) if reverse else (row >= dist)
            b = jnp.where(valid, a * b_sh + b, b)
            a = jnp.where(valid, a * a_sh, a)
        hcur = a * carry + b
        b_ref[pl.ds(r0, SUB), :] = hcur
        edge = hcur[0:1, :] if reverse else hcur[SUB - 1:SUB, :]
        return jnp.broadcast_to(edge, (SUB, ch))

    carry_ref[...] = lax.fori_loop(0, n_chunks, body, carry_ref[...], unroll=2)


def _rec_fwd_kernel(hp_ref, hm_ref, hn_ref, wg_ref, wx_ref, cw_ref, cb_ref, gw_ref, gab_ref, gxb_ref,
                    lam_ref, gate_ref, xc_ref, hf_ref, hh_ref, a_ref, b_ref, carry_ref, *, tile, seq):
    pos0 = lax.rem(pl.program_id(0) * tile, seq)
    _fill_halo(hh_ref, hp_ref, hm_ref, hn_ref, pos0, tile, seq)

    @pl.when(pos0 == 0)
    def _():
        carry_ref[...] = jnp.zeros(carry_ref.shape, F32)

    gate_ref[...] = _dot(hm_ref[...], wg_ref[...]).astype(BF16)
    left = (cw_ref.shape[0] - 1) // 2
    xc = _dwconv(_dot(hh_ref[...], wx_ref[...]), cw_ref[...], cb_ref[...], tile, left, chained_rolls=True)
    xc_bf = xc.astype(BF16)
    xc_ref[...] = xc_bf
    a, b = _lru_coeffs(xc, xc_bf, gw_ref, gab_ref, gxb_ref, lam_ref)
    a_ref[...] = a
    b_ref[...] = b
    _scan_tile(a_ref, b_ref, carry_ref, reverse=False)
    hf_ref[...] = b_ref[...].astype(BF16)


def _rec_fwd(h, w_g, w_x, conv_w, conv_b, gw, gab, gxb, lam, *, tile, seq):
    n, d = h.shape
    d_rnn = w_x.shape[1]
    tok = pl.BlockSpec((tile, d_rnn), lambda i: (i, 0))
    in_specs = _halo_specs(tile, d, n) + [_const_spec(a.shape) for a in
                                          (w_g, w_x, conv_w, conv_b, gw, gab, gxb, lam)]
    out = jax.ShapeDtypeStruct((n, d_rnn), BF16)
    return pl.pallas_call(
        functools.partial(_rec_fwd_kernel, tile=tile, seq=seq),
        grid=(n // tile,),
        in_specs=in_specs,
        out_specs=[tok, tok, tok],
        out_shape=[out, out, out],
        scratch_shapes=[pltpu.VMEM((tile + 2 * HALO, d), BF16), pltpu.VMEM((tile, d_rnn), F32),
                        pltpu.VMEM((tile, d_rnn), F32), pltpu.VMEM((SUB, d_rnn), F32)],
        compiler_params=_params(("arbitrary",)),
        name="rglru_fwd",
    )(h, h, h, w_g, w_x, conv_w, conv_b, gw, gab, gxb, lam)


def _rec_bwd_kernel(xc_ref, gate_ref, hf_ref, x_ref, gw_ref, gab_ref, gxb_ref, lam_ref, wo_ref,
                    gpost_ref, gnext_ref, xo_ref, ho_ref, a_ref, b_ref, carry_ref, *, tile, seq):
    i = pl.num_programs(0) - 1 - pl.program_id(0)
    pos0 = lax.rem(i * tile, seq)

    @pl.when(pos0 + tile == seq)
    def _():
        carry_ref[...] = jnp.zeros(carry_ref.shape, F32)

    xc_bf = xc_ref[...]
    a, b = _lru_coeffs(xc_bf.astype(F32), xc_bf, gw_ref, gab_ref, gxb_ref, lam_ref)
    a_ref[...] = a
    b_ref[...] = b
    _scan_tile(a_ref, b_ref, carry_ref, reverse=True)
    y = (hf_ref[...].astype(F32) + b_ref[...]) * _gelu_tanh(gate_ref[...].astype(F32))
    m = _dot(y.astype(BF16), wo_ref[...])
    xo = x_ref[...] + _rms(m, gpost_ref[...])
    xo_ref[...] = xo
    ho_ref[...] = _rms(xo, gnext_ref[...]).astype(BF16)


def _rec_bwd(xc, gate, hf, x, gw, gab, gxb, lam, w_out, g_post, g_next, *, tile, seq):
    n, d = x.shape
    d_rnn = xc.shape[1]
    nt = n // tile
    rtok = lambda w: pl.BlockSpec((tile, w), lambda i: (nt - 1 - i, 0))
    in_specs = [rtok(d_rnn), rtok(d_rnn), rtok(d_rnn), rtok(d)] + [
        _const_spec(a.shape) for a in (gw, gab, gxb, lam, w_out, g_post, g_next)]
    return pl.pallas_call(
        functools.partial(_rec_bwd_kernel, tile=tile, seq=seq),
        grid=(nt,),
        in_specs=in_specs,
        out_specs=[rtok(d), rtok(d)],
        out_shape=[jax.ShapeDtypeStruct((n, d), F32), jax.ShapeDtypeStruct((n, d), BF16)],
        scratch_shapes=[pltpu.VMEM((tile, d_rnn), F32), pltpu.VMEM((tile, d_rnn), F32),
                        pltpu.VMEM((SUB, d_rnn), F32)],
        compiler_params=_params(("arbitrary",)),
        name="rglru_bwd",
    )(xc, gate, hf, x, gw, gab, gxb, lam, w_out, g_post, g_next)


def _prenorm_kernel(x_ref, g_ref, h_ref):
    h_ref[...] = _rms(x_ref[...], g_ref[...]).astype(BF16)


def _prenorm(x, g, *, tile):
    n, d = x.shape
    tok = pl.BlockSpec((tile, d), lambda i: (i, 0))
    return pl.pallas_call(
        _prenorm_kernel, grid=(n // tile,), in_specs=[tok, _const_spec(g.shape)], out_specs=tok,
        out_shape=jax.ShapeDtypeStruct((n, d), BF16), compiler_params=_params(("parallel",)),
        name="prenorm",
    )(x, g)


def _row(v):
    return v.reshape(1, -1).astype(F32)


def _rope_tables(s_max):
    pos = jnp.arange(s_max, dtype=F32)
    inv = ROPE_BASE ** (-jnp.arange(0, MLA_ROPE, 2, dtype=F32) / MLA_ROPE)
    ang = pos[:, None] * inv[None, :]
    cos, sin = jnp.cos(ang), jnp.sin(ang)
    pad = LANES - MLA_NOPE - MLA_ROPE
    cos_t = jnp.concatenate([jnp.ones((s_max, MLA_NOPE), F32), cos, cos, jnp.zeros((s_max, pad), F32)], axis=1)
    sin_t = jnp.concatenate([jnp.zeros((s_max, MLA_NOPE), F32), -sin, sin, jnp.zeros((s_max, pad), F32)], axis=1)
    return cos_t, sin_t


def _attn_weights(w_in, w_uq, w_ukv):
    d = w_in.shape[0]
    q_lora = w_uq.shape[0]
    kv_lora = w_ukv.shape[0]
    o1 = q_lora + kv_lora
    o2 = o1 + MLA_ROPE
    pad = LANES - MLA_NOPE - MLA_ROPE
    kpe = w_in[:, o1:o2]
    zl = jnp.zeros((d, MLA_NOPE), w_in.dtype)
    zr = jnp.zeros((d, pad), w_in.dtype)
    w_all = jnp.concatenate([w_in[:, :o1], zl, kpe, zr, w_in[:, o2:]], axis=1)

    wq = w_uq.reshape(q_lora, MLA_HEADS, MLA_NOPE + MLA_ROPE)
    zq = jnp.zeros((q_lora, MLA_HEADS, pad), w_uq.dtype)
    wq_p = jnp.concatenate([wq, zq], axis=-1).reshape(q_lora, MLA_HEADS * LANES)

    wkv = w_ukv.reshape(kv_lora, MLA_HEADS, MLA_NOPE + MLA_V)
    zk = jnp.zeros((kv_lora, MLA_HEADS, LANES - MLA_NOPE), w_ukv.dtype)
    wk_p = jnp.concatenate([wkv[..., :MLA_NOPE], zk], axis=-1).reshape(kv_lora, MLA_HEADS * LANES)
    wv = wkv[..., MLA_NOPE:].reshape(kv_lora, MLA_HEADS * MLA_V)
    return tuple(a.astype(BF16) for a in (w_all, wq_p, wk_p, wv))


TOKEN_TILE = 1024
FFN_TILE = 1024
FFN_CHUNK = 1024
REC_TILE = 1024
MLA_SCORE_ELEMS = 2 ** 21


def kernel(x_prompt, x_sample, norm_mix_pre, norm_mix_post, norm_ffn_pre, norm_ffn_post, w_in_attn, q_norm, w_uq, kv_norm, w_ukv, na_rpb, w_out_attn, w_in_rec, conv_w_rec, conv_b_rec, gate_a_w, gate_a_b, gate_x_w, gate_x_b, lru_lambda, w_out_rec, w_ffn_up, conv_w_ffn, conv_b_ffn, w_ffn_down):
    depth = norm_mix_pre.shape[0]
    d_rnn = w_out_rec.shape[1]
    n_mla = MLA_HEADS * MLA_V
    cos_t, sin_t = _rope_tables(max(x_prompt.shape[1], x_sample.shape[1]))

    layers = []
    for li in range(depth):
        j = li // 2
        lw = dict(ffn_up=w_ffn_up[li].astype(BF16), ffn_down=w_ffn_down[li].astype(BF16),
                  ffn_conv_w=conv_w_ffn[li], ffn_conv_b=_row(conv_b_ffn[li]),
                  g_pre=_row(norm_mix_pre[li]), g_post=_row(norm_mix_post[li]),
                  g_ffn_pre=_row(norm_ffn_pre[li]), g_ffn_post=_row(norm_ffn_post[li]),
                  g_next=_row(norm_mix_pre[li + 1]) if li + 1 < depth else None)
        if li % 2 == 0:
            lw["attn"] = _attn_weights(w_in_attn[j], w_uq[j], w_ukv[j])
            lw["q_norm"] = _row(q_norm[j])
            lw["kv_norm"] = _row(kv_norm[j])
            lw["bias"] = _na_bias_table(na_rpb[j])
            w_o = w_out_attn[j].astype(BF16)
            lw["out"] = (w_o[:n_mla], w_o[n_mla:])
        else:
            w_in = w_in_rec[j].astype(BF16)
            lw["rec_in"] = (w_in[:, :d_rnn], w_in[:, d_rnn:])
            lw["rec_conv"] = (conv_w_rec[j], _row(conv_b_rec[j]))
            lw["dirs"] = [((0.5 * jnp.concatenate([gate_a_w[j, e], gate_x_w[j, e]], axis=-1)).astype(BF16),
                           _row(gate_a_b[j, e]), _row(gate_x_b[j, e]), _row(lru_lambda[j, e])) for e in range(2)]
            lw["rec_out"] = w_out_rec[j].astype(BF16)
        layers.append(lw)

    def trunk(x3):
        batch, seq, d = x3.shape
        x = x3.reshape(batch * seq, d)
        tile = min(TOKEN_TILE, seq)
        h = None
        for li, lw in enumerate(layers):
            if li % 2 == 0:
                w_all, wq_p, wk_p, wv = lw["attn"]
                q, k, v, naq, nak, nav = _attn_in(x, lw["g_pre"], w_all, lw["q_norm"], wq_p, lw["kv_norm"], wk_p, wv,
                                                  cos_t, sin_t, tile=tile, seq=seq)
                o_mla = _mla(q, k, v, batch=batch, seq=seq, tq=min(seq, MLA_SCORE_ELEMS // seq))
                o_na = _na(naq, nak, nav, lw["bias"], batch=batch, seq=seq)
                x, h = _attn_out(o_mla, o_na, lw["out"][0], lw["out"][1], x, lw["g_post"], lw["g_ffn_pre"], tile=tile)
            else:
                if h is None:
                    h = _prenorm(x, lw["g_pre"], tile=tile)
                rec_tile = min(REC_TILE, seq)
                gate, xc, hf = _rec_fwd(h, lw["rec_in"][0], lw["rec_in"][1], *lw["rec_conv"], *lw["dirs"][0],
                                        tile=rec_tile, seq=seq)
                x, h = _rec_bwd(xc, gate, hf, x, *lw["dirs"][1], lw["rec_out"], lw["g_post"], lw["g_ffn_pre"],
                                tile=rec_tile, seq=seq)
            x, h = _ffn(h, x, lw["ffn_up"], lw["ffn_conv_w"], lw["ffn_conv_b"], lw["ffn_down"], lw["g_ffn_post"],
                        lw["g_next"], tile=min(FFN_TILE, seq), fc=min(FFN_CHUNK, lw["ffn_down"].shape[0]), seq=seq)
        return x.reshape(batch, seq, d)

    return trunk(x_prompt), trunk(x_sample)
```

```python
import functools
import math

import numpy as np
import jax
import jax.numpy as jnp
from jax import lax
from jax.experimental import pallas as pl
from jax.experimental.pallas import tpu as pltpu

F32 = jnp.float32
BF16 = jnp.bfloat16

EPS = 1e-6
GRID_W = 64
MLA_HEADS = 8
MLA_NOPE = 64
MLA_ROPE = 32
MLA_V = 64
ROPE_BASE = 10000.0
NA_HEADS = 8
NA_HD = 64
NA_KH = 8
NA_KW = 16
LRU_C = 8.0

LANES = 128
SUB = 8
HALO = 16
NEG = -1e30
VMEM_LIMIT = 56 * 1024 * 1024
LOG2E = math.log2(math.e)


def _rms(x, g):
    return x * lax.rsqrt(jnp.mean(x * x, axis=-1, keepdims=True) + EPS) * g


def _gelu_tanh(x):
    c1 = math.sqrt(2.0 / math.pi)
    k1 = -2.0 * c1 * LOG2E
    k2 = k1 * 0.044715
    e = jnp.exp2(x * ((x * x) * k2 + k1))
    return x / (1.0 + e)


def _dot(a, b):
    return jnp.dot(a, b, preferred_element_type=F32)


def _dot_nt(a, b):
    return lax.dot_general(a, b, (((1,), (1,)), ((), ())), preferred_element_type=F32)


def _params(sem):
    return pltpu.CompilerParams(dimension_semantics=sem, vmem_limit_bytes=VMEM_LIMIT)


def _const_spec(shape):
    nd = len(shape)
    return pl.BlockSpec(shape, lambda *_: (0,) * nd)


def _rope(x, cos, sin):
    half = MLA_ROPE // 2
    lane = lax.broadcasted_iota(jnp.int32, x.shape, 1)
    partner = jnp.where(lane < MLA_NOPE + half, pltpu.roll(x, LANES - half, axis=1), pltpu.roll(x, half, axis=1))
    return x * cos + partner * sin


def _attn_in_kernel(x_ref, g_ref, win_ref, qn_ref, wq_ref, kvn_ref, wk_ref, wv_ref,
                    cos_ref, sin_ref,
                    q_ref, k_ref, v_ref, naq_ref, nak_ref, nav_ref, *, q_lora, kv_lora, q_scale, na_scale):
    h = _rms(x_ref[...], g_ref[...]).astype(BF16)
    z = _dot(h, win_ref[...])
    o0 = q_lora
    o1 = o0 + kv_lora
    o3 = o1 + LANES
    na_w = NA_HEADS * NA_HD
    cos = cos_ref[...]
    sin = sin_ref[...]

    nq = _rms(z[:, :o0], qn_ref[...]).astype(BF16)
    q = _dot(nq, wq_ref[...])
    for hd in range(MLA_HEADS):
        sl = slice(hd * LANES, (hd + 1) * LANES)
        q_ref[:, sl] = (_rope(q[:, sl], cos, sin) * q_scale).astype(BF16)

    nkv = _rms(z[:, o0:o1], kvn_ref[...]).astype(BF16)
    k = _dot(nkv, wk_ref[...])
    rot = _rope(z[:, o1:o3], cos, sin)
    for hd in range(MLA_HEADS):
        sl = slice(hd * LANES, (hd + 1) * LANES)
        k_ref[:, sl] = (k[:, sl] + rot).astype(BF16)
    v_ref[...] = _dot(nkv, wv_ref[...]).astype(BF16)

    naq_ref[...] = (z[:, o3:o3 + na_w] * na_scale).astype(BF16)
    nak_ref[...] = z[:, o3 + na_w:o3 + 2 * na_w].astype(BF16)
    nav_ref[...] = z[:, o3 + 2 * na_w:o3 + 3 * na_w].astype(BF16)


def _attn_in(x, g, w_all, qn, wq, kvn, wk, wv, cos_t, sin_t, *, tile, seq):
    n, d = x.shape
    q_lora = qn.shape[1]
    kv_lora = kvn.shape[1]
    hw = MLA_HEADS * LANES
    vw = MLA_HEADS * MLA_V
    na_w = NA_HEADS * NA_HD
    per_seq = seq // tile

    tok = lambda w: pl.BlockSpec((tile, w), lambda i: (i, 0))
    rope = pl.BlockSpec((tile, LANES), lambda i: (lax.rem(i, per_seq), 0))
    kern = functools.partial(_attn_in_kernel, q_lora=q_lora, kv_lora=kv_lora,
                             q_scale=float((MLA_NOPE + MLA_ROPE) ** -0.5 * LOG2E),
                             na_scale=float(NA_HD ** -0.5 * LOG2E))
    return pl.pallas_call(
        kern,
        grid=(n // tile,),
        in_specs=[tok(d), _const_spec(g.shape), _const_spec(w_all.shape), _const_spec(qn.shape),
                  _const_spec(wq.shape), _const_spec(kvn.shape),
                  _const_spec(wk.shape), _const_spec(wv.shape), rope, rope],
        out_specs=[tok(hw), tok(hw), tok(vw), tok(na_w), tok(na_w), tok(na_w)],
        out_shape=[jax.ShapeDtypeStruct((n, hw), BF16), jax.ShapeDtypeStruct((n, hw), BF16),
                   jax.ShapeDtypeStruct((n, vw), BF16),
                   jax.ShapeDtypeStruct((n, na_w), BF16), jax.ShapeDtypeStruct((n, na_w), BF16),
                   jax.ShapeDtypeStruct((n, na_w), BF16)],
        compiler_params=_params(("parallel",)),
        name="attn_in",
    )(x, g, w_all, qn, wq, kvn, wk, wv, cos_t, sin_t)


def _mla_kernel(q_ref, k_ref, v_ref, o_ref):
    tq = q_ref.shape[0]
    seq = k_ref.shape[0]
    heads = [slice(hd * LANES, (hd + 1) * LANES) for hd in range(2)]
    scores = [_dot_nt(q_ref[:, hsl], k_ref[:, hsl]) for hsl in heads]
    probs = [jnp.exp2((s - jnp.max(s, axis=-1, keepdims=True)).astype(BF16)) for s in scores]
    v_ext = jnp.concatenate([v_ref[...], jnp.ones((seq, LANES), BF16)], axis=1)
    accs = [_dot(p, v_ext) for p in probs]
    outs = [acc[:, :LANES] / acc[:, LANES:LANES + 1] for acc in accs]
    lane = lax.broadcasted_iota(jnp.int32, (tq, LANES), 1)
    o_ref[...] = jnp.where(lane < MLA_V, outs[0], outs[1]).astype(BF16)


def _mla(q, k, v, *, batch, seq, tq):
    n = q.shape[0]
    nq = seq // tq
    return pl.pallas_call(
        _mla_kernel,
        grid=(batch, MLA_HEADS // 2, nq),
        in_specs=[pl.BlockSpec((tq, 2 * LANES), lambda b, hp, i: (b * nq + i, hp)),
                  pl.BlockSpec((seq, 2 * LANES), lambda b, hp, i: (b, hp)),
                  pl.BlockSpec((seq, LANES), lambda b, hp, i: (b, hp))],
        out_specs=pl.BlockSpec((tq, LANES), lambda b, hp, i: (b * nq + i, hp)),
        out_shape=jax.ShapeDtypeStruct((n, MLA_HEADS * MLA_V), BF16),
        compiler_params=_params(("parallel", "parallel", "arbitrary")),
        name="mla_attn",
    )(q, k, v)


NA_GROUP = 32


def _na_kernel(q_ref, k_ref, v_ref, bias_ref, o_ref, *, rows):
    win = NA_KH * GRID_W
    lane_q = lax.broadcasted_iota(jnp.int32, (GRID_W, LANES), 1)
    ones = jnp.ones((win, LANES), BF16)
    group = math.gcd(NA_GROUP, rows)

    def body(g, _):
        q0s, k0s, scores = [], [], []
        for j in range(group):
            r = g * group + j
            rs = jnp.clip(r - NA_KH // 2, 0, rows - NA_KH)
            q0 = pl.multiple_of(r * GRID_W, GRID_W)
            k0 = pl.multiple_of(rs * GRID_W, GRID_W)
            qr = q_ref[pl.ds(q0, GRID_W), :]
            zero = jnp.zeros_like(qr)
            q2 = jnp.concatenate([jnp.where(lane_q < NA_HD, qr, zero), jnp.where(lane_q >= NA_HD, qr, zero)],
                                 axis=0)
            scores.append(_dot_nt(q2, k_ref[pl.ds(k0, win), :]) + bias_ref[r - rs])
            q0s.append(q0)
            k0s.append(k0)
        probs = [jnp.exp2((s - jnp.max(s, axis=-1, keepdims=True)).astype(BF16)) for s in scores]
        accs = [_dot(p, jnp.concatenate([v_ref[pl.ds(k0, win), :], ones], axis=1)) for p, k0 in zip(probs, k0s)]
        for q0, acc in zip(q0s, accs):
            o2 = acc[:, :LANES] / acc[:, LANES:LANES + 1]
            o_ref[pl.ds(q0, GRID_W), :] = jnp.where(lane_q < NA_HD, o2[:GRID_W], o2[GRID_W:]).astype(BF16)
        return 0

    lax.fori_loop(0, rows // group, body, 0)


def _na(q, k, v, bias, *, batch, seq):
    n = q.shape[0]
    rows = seq // GRID_W
    blk = pl.BlockSpec((seq, LANES), lambda b, hp: (b, hp))
    return pl.pallas_call(
        functools.partial(_na_kernel, rows=rows),
        grid=(batch, NA_HEADS // 2),
        in_specs=[blk, blk, blk, pl.BlockSpec((None,) + bias.shape[1:], lambda b, hp: (hp, 0, 0, 0))],
        out_specs=blk,
        out_shape=jax.ShapeDtypeStruct((n, NA_HEADS * NA_HD), BF16),
        compiler_params=_params(("parallel", "parallel")),
        name="na_attn",
    )(q, k, v, bias)


def _na_bias_table(rpb):
    n_rel = 2 * NA_KW - 1
    period = 2 * GRID_W
    left = GRID_W - NA_KW
    p = jnp.pad(rpb.astype(F32), ((0, 0), (0, 0), (left, period - n_rel - left)))
    lead = p.shape[:-1]
    flat = jnp.broadcast_to(p[..., None, :], lead + (GRID_W, period)).reshape(lead + (GRID_W * period,))
    skew = flat[..., :GRID_W * (period - 1)].reshape(lead + (GRID_W, period - 1))
    by_col = skew[..., GRID_W - 1:2 * GRID_W - 1]
    cols = np.arange(GRID_W)
    col_start = np.clip(cols - NA_KW // 2, 0, GRID_W - NA_KW)
    kc = np.arange(GRID_W)
    in_win = (kc[None, :] >= col_start[:, None]) & (kc[None, :] < col_start[:, None] + NA_KW)
    by_col = jnp.where(jnp.asarray(in_win)[None, None], by_col * LOG2E, NEG)
    t = jnp.stack([by_col[:, NA_KH - 1 - dl:2 * NA_KH - 1 - dl] for dl in range(NA_KH)], axis=1)
    t = jnp.swapaxes(t, 2, 3).reshape(rpb.shape[0] // 2, 2, NA_KH, GRID_W, NA_KH * GRID_W)
    return jnp.swapaxes(t, 1, 2).reshape(rpb.shape[0] // 2, NA_KH, 2 * GRID_W, NA_KH * GRID_W)


def _to_slabs(val, nat_ref):
    rows = val.shape[0]
    s = rows // SUB
    ng = nat_ref.shape[0]
    for k in range(ng):
        nat_ref[k] = val[:, k * LANES:(k + 1) * LANES].astype(F32)
    return jnp.concatenate(
        [jnp.concatenate([nat_ref[k, pl.ds(r, s, stride=SUB), :] for k in range(ng)], axis=1) for r in range(SUB)],
        axis=0)


def _attn_out_kernel(om_ref, on_ref, wm_ref, wn_ref, x_ref, gpost_ref, gnext_ref, xo_ref, ho_ref, nat_ref):
    m = _dot(om_ref[...], wm_ref[...]) + _dot(on_ref[...], wn_ref[...])
    xo = x_ref[...] + _rms(m, gpost_ref[...])
    xo_ref[...] = xo
    ho_ref[...] = _to_slabs(_rms(xo, gnext_ref[...]), nat_ref).astype(BF16)


def _attn_out(o_mla, o_na, w_m, w_n, x, g_post, g_next, *, tile):
    n, d = x.shape
    tok = lambda w: pl.BlockSpec((tile, w), lambda i: (i, 0))
    return pl.pallas_call(
        _attn_out_kernel,
        grid=(n // tile,),
        in_specs=[tok(o_mla.shape[1]), tok(o_na.shape[1]), _const_spec(w_m.shape), _const_spec(w_n.shape),
                  tok(d), _const_spec(g_post.shape), _const_spec(g_next.shape)],
        out_specs=[tok(d), tok(d)],
        out_shape=[jax.ShapeDtypeStruct((n, d), F32), jax.ShapeDtypeStruct((n, d), BF16)],
        scratch_shapes=[pltpu.VMEM((d // LANES, tile, LANES), F32)],
        compiler_params=_params(("parallel",)),
        name="attn_out",
    )(o_mla, o_na, w_m, w_n, x, g_post, g_next)


def _halo_specs(tile, d, n):
    per = tile // HALO
    last = n // HALO - 1
    return [pl.BlockSpec((HALO, d), lambda i, *_: (jnp.maximum(i * per - 1, 0), 0)),
            pl.BlockSpec((tile, d), lambda i, *_: (i, 0)),
            pl.BlockSpec((HALO, d), lambda i, *_: (jnp.minimum((i + 1) * per, last), 0))]


def _fill_halo(hh_ref, hp_ref, hm_ref, hn_ref, pos0, tile, seq):
    zero = jnp.zeros(hp_ref.shape, hp_ref.dtype)
    hh_ref[0:HALO, :] = jnp.where(pos0 == 0, zero, hp_ref[...])
    hh_ref[HALO:HALO + tile, :] = hm_ref[...]
    hh_ref[HALO + tile:, :] = jnp.where(pos0 + tile == seq, zero, hn_ref[...])


def _dwconv(u, w, b, tile, left, chained_rolls=False):
    taps = w.shape[0]
    if chained_rolls:
        n = u.shape[0]
        shifted = {0: u}
        for d in range(1, taps - left):
            shifted[d] = pltpu.roll(shifted[d - 1], n - 1, axis=0)
        for d in range(1, left + 1):
            shifted[-d] = pltpu.roll(shifted[1 - d], 1, axis=0)
        y = b
        for kk in range(taps):
            y = y + w[kk:kk + 1, :] * shifted[kk - left][HALO:HALO + tile, :]
        return y
    y = b
    for kk in range(taps):
        s = HALO - left + kk
        y = y + w[kk:kk + 1, :] * u[s:s + tile, :]
    return y


def _slab_conv3(u, w, b, tile):
    s = tile // SUB
    slabs = [u[r * s:(r + 1) * s, :] for r in range(SUB)]
    row = lax.broadcasted_iota(jnp.int32, slabs[0].shape, 0)
    before = jnp.where(row == 0, u[tile:tile + 1, :], pltpu.roll(slabs[SUB - 1], 1, axis=0))
    after = jnp.where(row == s - 1, u[tile + 1:tile + 2, :], pltpu.roll(slabs[0], s - 1, axis=0))
    out = []
    for r in range(SUB):
        left = slabs[r - 1] if r > 0 else before
        right = slabs[r + 1] if r < SUB - 1 else after
        out.append(b + w[0:1, :] * left + w[1:2, :] * slabs[r] + w[2:3, :] * right)
    return jnp.concatenate(out, axis=0)


def _ffn_kernel(*refs, tile, seq, emit_next, slab_in):
    if emit_next:
        (hp_ref, hm_ref, hn_ref, x_ref, wg_ref, wv_ref, cwg_ref, cwv_ref, cbg_ref, cbv_ref, wd_ref,
         gpost_ref, gnext_ref, xo_ref, ho_ref, hh_ref, acc_ref, nat_ref) = refs
    else:
        (hp_ref, hm_ref, hn_ref, x_ref, wg_ref, wv_ref, cwg_ref, cwv_ref, cbg_ref, cbv_ref, wd_ref,
         gpost_ref, xo_ref, hh_ref, acc_ref, nat_ref) = refs
    i = pl.program_id(0)
    c = pl.program_id(1)
    s = tile // SUB
    ng = nat_ref.shape[0]

    @pl.when(c == 0)
    def _():
        pos0 = lax.rem(i * tile, seq)
        if slab_in:
            hh_ref[0:tile, :] = hm_ref[...]
        else:
            hh_ref[0:tile, :] = _to_slabs(hm_ref[...], nat_ref).astype(BF16)
        zero = jnp.zeros((1, hm_ref.shape[1]), F32)
        prev = jnp.where(pos0 == 0, zero, hp_ref[HALO - 1:HALO, :].astype(F32))
        nxt = jnp.where(pos0 + tile == seq, zero, hn_ref[0:1, :].astype(F32))
        tail = jnp.concatenate([prev, nxt, jnp.zeros((HALO - 2, hm_ref.shape[1]), F32)], axis=0)
        hh_ref[tile:, :] = tail.astype(BF16)
        acc_ref[...] = jnp.zeros(acc_ref.shape, F32)

    hh = hh_ref[...]
    g = _slab_conv3(_dot(hh, wg_ref[...]), cwg_ref[...], cbg_ref[...], tile)
    val = _slab_conv3(_dot(hh, wv_ref[...]), cwv_ref[...], cbv_ref[...], tile)
    a = (_gelu_tanh(g) * val).astype(BF16)
    acc_ref[...] += _dot(a, wd_ref[...])

    @pl.when(c == pl.num_programs(1) - 1)
    def _():
        f = _rms(acc_ref[...], gpost_ref[...])
        for r in range(SUB):
            for k in range(ng):
                nat_ref[k, pl.ds(r, s, stride=SUB), :] = f[r * s:(r + 1) * s, k * LANES:(k + 1) * LANES]
        xo = x_ref[...] + jnp.concatenate([nat_ref[k] for k in range(ng)], axis=1)
        xo_ref[...] = xo
        if emit_next:
            ho_ref[...] = _rms(xo, gnext_ref[...]).astype(BF16)


def _ffn(h, x, w_up, conv_w, conv_b, w_down, g_post, g_next, *, tile, fc, seq, slab_in):
    n, d = x.shape
    d_ff = w_down.shape[0]
    nc = d_ff // fc
    kk = conv_w.shape[0]
    assert kk == 3 and d % LANES == 0 and tile % (SUB * SUB) == 0
    emit_next = g_next is not None
    tok = pl.BlockSpec((tile, d), lambda i, c: (i, 0))
    in_specs = _halo_specs(tile, d, n) + [
        tok,
        pl.BlockSpec((d, fc), lambda i, c: (0, c)),
        pl.BlockSpec((d, fc), lambda i, c: (0, nc + c)),
        pl.BlockSpec((kk, fc), lambda i, c: (0, c)),
        pl.BlockSpec((kk, fc), lambda i, c: (0, nc + c)),
        pl.BlockSpec((1, fc), lambda i, c: (0, c)),
        pl.BlockSpec((1, fc), lambda i, c: (0, nc + c)),
        pl.BlockSpec((fc, d), lambda i, c: (c, 0)),
        _const_spec(g_post.shape)]
    args = [h, h, h, x, w_up, w_up, conv_w, conv_w, conv_b, conv_b, w_down, g_post]
    out_specs = [tok]
    out_shape = [jax.ShapeDtypeStruct((n, d), F32)]
    if emit_next:
        in_specs.append(_const_spec(g_next.shape))
        args.append(g_next)
        out_specs.append(tok)
        out_shape.append(jax.ShapeDtypeStruct((n, d), BF16))
    res = pl.pallas_call(
        functools.partial(_ffn_kernel, tile=tile, seq=seq, emit_next=emit_next, slab_in=slab_in),
        grid=(n // tile, nc),
        in_specs=in_specs,
        out_specs=out_specs,
        out_shape=out_shape,
        scratch_shapes=[pltpu.VMEM((tile + HALO, d), BF16), pltpu.VMEM((tile, d), F32),
                        pltpu.VMEM((d // LANES, tile, LANES), F32)],
        compiler_params=_params(("parallel", "arbitrary")),
        name="conv_ffn",
    )(*args)
    return (res[0], res[1]) if emit_next else (res[0], None)


def _lru_coeffs(xc, xc_bf, gw_ref, gab_ref, gxb_ref, lam_ref):
    nb, bw, _ = gw_ref.shape
    neg = -lam_ref[...]
    sp = jnp.maximum(neg, 0.0) + jnp.log1p(jnp.exp(-jnp.abs(neg)))
    half_c = (-0.5 * LRU_C) * sp
    half_gab = 0.5 * gab_ref[...]
    half_gxb = 0.5 * gxb_ref[...]
    a_parts, b_parts = [], []
    for blk in range(nb):
        sl = slice(blk * bw, (blk + 1) * bw)
        rg = _dot(xc_bf[:, sl], gw_ref[blk])
        hc = half_c[:, sl]
        log_a = hc * jnp.tanh(rg[:, :bw] + half_gab[:, sl]) + hc
        hx = 0.5 * xc[:, sl]
        gated_x = hx * jnp.tanh(rg[:, bw:] + half_gxb[:, sl]) + hx
        t = jnp.tanh(log_a)
        one_minus_a2 = (-2.0 * t) / (1.0 - t)
        root = jnp.where(one_minus_a2 > 0.0, one_minus_a2 * lax.rsqrt(one_minus_a2), 0.0)
        a_parts.append(jnp.exp2(log_a * LOG2E))
        b_parts.append(root * gated_x)
    return jnp.concatenate(a_parts, axis=1), jnp.concatenate(b_parts, axis=1)


def _scan_tile(a_ref, b_ref, carry_ref, *, reverse):
    t, ch = a_ref.shape
    n_chunks = t // SUB
    row = lax.broadcasted_iota(jnp.int32, (SUB, ch), 0)

    def body(j, carry):
        jj = (n_chunks - 1 - j) if reverse else j
        r0 = pl.multiple_of(jj * SUB, SUB)
        a = a_ref[pl.ds(r0, SUB), :]
        b = b_ref[pl.ds(r0, SUB), :]
        for dist in (1, 2, 4):
            shift = (SUB - dist) if reverse else dist
            a_sh = pltpu.roll(a, shift, axis=0)
            b_sh = pltpu.roll(b, shift, axis=0)
            valid = (row < SUB - dist) if reverse else (row >= dist)
            b = jnp.where(valid, a * b_sh + b, b)
            a = jnp.where(valid, a * a_sh, a)
        hcur = a * carry + b
        b_ref[pl.ds(r0, SUB), :] = hcur
        edge = hcur[0:1, :] if reverse else hcur[SUB - 1:SUB, :]
        return jnp.broadcast_to(edge, (SUB, ch))

    carry_ref[...] = lax.fori_loop(0, n_chunks, body, carry_ref[...], unroll=2)


def _rec_fwd_kernel(hp_ref, hm_ref, hn_ref, wg_ref, wx_ref, cw_ref, cb_ref, gw_ref, gab_ref, gxb_ref,
                    lam_ref, gate_ref, xc_ref, hf_ref, hh_ref, a_ref, b_ref, carry_ref, *, tile, seq):
    pos0 = lax.rem(pl.program_id(0) * tile, seq)
    _fill_halo(hh_ref, hp_ref, hm_ref, hn_ref, pos0, tile, seq)

    @pl.when(pos0 == 0)
    def _():
        carry_ref[...] = jnp.zeros(carry_ref.shape, F32)

    gate_ref[...] = _dot(hm_ref[...], wg_ref[...]).astype(BF16)
    left = (cw_ref.shape[0] - 1) // 2
    xc = _dwconv(_dot(hh_ref[...], wx_ref[...]), cw_ref[...], cb_ref[...], tile, left, chained_rolls=True)
    xc_bf = xc.astype(BF16)
    xc_ref[...] = xc_bf
    a, b = _lru_coeffs(xc, xc_bf, gw_ref, gab_ref, gxb_ref, lam_ref)
    a_ref[...] = a
    b_ref[...] = b
    _scan_tile(a_ref, b_ref, carry_ref, reverse=False)
    hf_ref[...] = b_ref[...].astype(BF16)


def _rec_fwd(h, w_g, w_x, conv_w, conv_b, gw, gab, gxb, lam, *, tile, seq):
    n, d = h.shape
    d_rnn = w_x.shape[1]
    tok = pl.BlockSpec((tile, d_rnn), lambda i: (i, 0))
    in_specs = _halo_specs(tile, d, n) + [_const_spec(a.shape) for a in
                                          (w_g, w_x, conv_w, conv_b, gw, gab, gxb, lam)]
    out = jax.ShapeDtypeStruct((n, d_rnn), BF16)
    return pl.pallas_call(
        functools.partial(_rec_fwd_kernel, tile=tile, seq=seq),
        grid=(n // tile,),
        in_specs=in_specs,
        out_specs=[tok, tok, tok],
        out_shape=[out, out, out],
        scratch_shapes=[pltpu.VMEM((tile + 2 * HALO, d), BF16), pltpu.VMEM((tile, d_rnn), F32),
                        pltpu.VMEM((tile, d_rnn), F32), pltpu.VMEM((SUB, d_rnn), F32)],
        compiler_params=_params(("arbitrary",)),
        name="rglru_fwd",
    )(h, h, h, w_g, w_x, conv_w, conv_b, gw, gab, gxb, lam)


def _rec_bwd_kernel(xc_ref, gate_ref, hf_ref, x_ref, gw_ref, gab_ref, gxb_ref, lam_ref, wo_ref,
                    gpost_ref, gnext_ref, xo_ref, ho_ref, a_ref, b_ref, carry_ref, *, tile, seq):
    i = pl.num_programs(0) - 1 - pl.program_id(0)
    pos0 = lax.rem(i * tile, seq)

    @pl.when(pos0 + tile == seq)
    def _():
        carry_ref[...] = jnp.zeros(carry_ref.shape, F32)

    xc_bf = xc_ref[...]
    a, b = _lru_coeffs(xc_bf.astype(F32), xc_bf, gw_ref, gab_ref, gxb_ref, lam_ref)
    a_ref[...] = a
    b_ref[...] = b
    _scan_tile(a_ref, b_ref, carry_ref, reverse=True)
    y = (hf_ref[...].astype(F32) + b_ref[...]) * _gelu_tanh(gate_ref[...].astype(F32))
    m = _dot(y.astype(BF16), wo_ref[...])
    xo = x_ref[...] + _rms(m, gpost_ref[...])
    xo_ref[...] = xo
    ho_ref[...] = _rms(xo, gnext_ref[...]).astype(BF16)


def _rec_bwd(xc, gate, hf, x, gw, gab, gxb, lam, w_out, g_post, g_next, *, tile, seq):
    n, d = x.shape
    d_rnn = xc.shape[1]
    nt = n // tile
    rtok = lambda w: pl.BlockSpec((tile, w), lambda i: (nt - 1 - i, 0))
    in_specs = [rtok(d_rnn), rtok(d_rnn), rtok(d_rnn), rtok(d)] + [
        _const_spec(a.shape) for a in (gw, gab, gxb, lam, w_out, g_post, g_next)]
    return pl.pallas_call(
        functools.partial(_rec_bwd_kernel, tile=tile, seq=seq),
        grid=(nt,),
        in_specs=in_specs,
        out_specs=[rtok(d), rtok(d)],
        out_shape=[jax.ShapeDtypeStruct((n, d), F32), jax.ShapeDtypeStruct((n, d), BF16)],
        scratch_shapes=[pltpu.VMEM((tile, d_rnn), F32), pltpu.VMEM((tile, d_rnn), F32),
                        pltpu.VMEM((SUB, d_rnn), F32)],
        compiler_params=_params(("arbitrary",)),
        name="rglru_bwd",
    )(xc, gate, hf, x, gw, gab, gxb, lam, w_out, g_post, g_next)


def _prenorm_kernel(x_ref, g_ref, h_ref):
    h_ref[...] = _rms(x_ref[...], g_ref[...]).astype(BF16)


def _prenorm(x, g, *, tile):
    n, d = x.shape
    tok = pl.BlockSpec((tile, d), lambda i: (i, 0))
    return pl.pallas_call(
        _prenorm_kernel, grid=(n // tile,), in_specs=[tok, _const_spec(g.shape)], out_specs=tok,
        out_shape=jax.ShapeDtypeStruct((n, d), BF16), compiler_params=_params(("parallel",)),
        name="prenorm",
    )(x, g)


def _row(v):
    return v.reshape(1, -1).astype(F32)


def _rope_tables(s_max):
    pos = jnp.arange(s_max, dtype=F32)
    inv = ROPE_BASE ** (-jnp.arange(0, MLA_ROPE, 2, dtype=F32) / MLA_ROPE)
    ang = pos[:, None] * inv[None, :]
    cos, sin = jnp.cos(ang), jnp.sin(ang)
    pad = LANES - MLA_NOPE - MLA_ROPE
    cos_t = jnp.concatenate([jnp.ones((s_max, MLA_NOPE), F32), cos, cos, jnp.zeros((s_max, pad), F32)], axis=1)
    sin_t = jnp.concatenate([jnp.zeros((s_max, MLA_NOPE), F32), -sin, sin, jnp.zeros((s_max, pad), F32)], axis=1)
    return cos_t, sin_t


def _attn_weights(w_in, w_uq, w_ukv):
    d = w_in.shape[0]
    q_lora = w_uq.shape[0]
    kv_lora = w_ukv.shape[0]
    o1 = q_lora + kv_lora
    o2 = o1 + MLA_ROPE
    pad = LANES - MLA_NOPE - MLA_ROPE
    kpe = w_in[:, o1:o2]
    zl = jnp.zeros((d, MLA_NOPE), w_in.dtype)
    zr = jnp.zeros((d, pad), w_in.dtype)
    w_all = jnp.concatenate([w_in[:, :o1], zl, kpe, zr, w_in[:, o2:]], axis=1)

    wq = w_uq.reshape(q_lora, MLA_HEADS, MLA_NOPE + MLA_ROPE)
    zq = jnp.zeros((q_lora, MLA_HEADS, pad), w_uq.dtype)
    wq_p = jnp.concatenate([wq, zq], axis=-1).reshape(q_lora, MLA_HEADS * LANES)

    wkv = w_ukv.reshape(kv_lora, MLA_HEADS, MLA_NOPE + MLA_V)
    zk = jnp.zeros((kv_lora, MLA_HEADS, LANES - MLA_NOPE), w_ukv.dtype)
    wk_p = jnp.concatenate([wkv[..., :MLA_NOPE], zk], axis=-1).reshape(kv_lora, MLA_HEADS * LANES)
    wv = wkv[..., MLA_NOPE:].reshape(kv_lora, MLA_HEADS * MLA_V)
    return tuple(a.astype(BF16) for a in (w_all, wq_p, wk_p, wv))


TOKEN_TILE = 1024
FFN_TILE = 1024
FFN_CHUNK = 1024
REC_TILE = 1024
MLA_SCORE_ELEMS = 2 ** 21


def kernel(x_prompt, x_sample, norm_mix_pre, norm_mix_post, norm_ffn_pre, norm_ffn_post, w_in_attn, q_norm, w_uq, kv_norm, w_ukv, na_rpb, w_out_attn, w_in_rec, conv_w_rec, conv_b_rec, gate_a_w, gate_a_b, gate_x_w, gate_x_b, lru_lambda, w_out_rec, w_ffn_up, conv_w_ffn, conv_b_ffn, w_ffn_down):
    depth = norm_mix_pre.shape[0]
    d_rnn = w_out_rec.shape[1]
    n_mla = MLA_HEADS * MLA_V
    cos_t, sin_t = _rope_tables(max(x_prompt.shape[1], x_sample.shape[1]))

    layers = []
    for li in range(depth):
        j = li // 2
        lw = dict(ffn_up=w_ffn_up[li].astype(BF16), ffn_down=w_ffn_down[li].astype(BF16),
                  ffn_conv_w=conv_w_ffn[li], ffn_conv_b=_row(conv_b_ffn[li]),
                  g_pre=_row(norm_mix_pre[li]), g_post=_row(norm_mix_post[li]),
                  g_ffn_pre=_row(norm_ffn_pre[li]), g_ffn_post=_row(norm_ffn_post[li]),
                  g_next=_row(norm_mix_pre[li + 1]) if li + 1 < depth else None)
        if li % 2 == 0:
            lw["attn"] = _attn_weights(w_in_attn[j], w_uq[j], w_ukv[j])
            lw["q_norm"] = _row(q_norm[j])
            lw["kv_norm"] = _row(kv_norm[j])
            lw["bias"] = _na_bias_table(na_rpb[j])
            w_o = w_out_attn[j].astype(BF16)
            lw["out"] = (w_o[:n_mla], w_o[n_mla:])
        else:
            w_in = w_in_rec[j].astype(BF16)
            lw["rec_in"] = (w_in[:, :d_rnn], w_in[:, d_rnn:])
            lw["rec_conv"] = (conv_w_rec[j], _row(conv_b_rec[j]))
            lw["dirs"] = [((0.5 * jnp.concatenate([gate_a_w[j, e], gate_x_w[j, e]], axis=-1)).astype(BF16),
                           _row(gate_a_b[j, e]), _row(gate_x_b[j, e]), _row(lru_lambda[j, e])) for e in range(2)]
            lw["rec_out"] = w_out_rec[j].astype(BF16)
        layers.append(lw)

    def trunk(x3):
        batch, seq, d = x3.shape
        x = x3.reshape(batch * seq, d)
        tile = min(TOKEN_TILE, seq)
        ffn_tile = min(FFN_TILE, seq)
        h = None
        for li, lw in enumerate(layers):
            if li % 2 == 0:
                w_all, wq_p, wk_p, wv = lw["attn"]
                q, k, v, naq, nak, nav = _attn_in(x, lw["g_pre"], w_all, lw["q_norm"], wq_p, lw["kv_norm"], wk_p, wv,
                                                  cos_t, sin_t, tile=tile, seq=seq)
                o_mla = _mla(q, k, v, batch=batch, seq=seq, tq=min(seq, MLA_SCORE_ELEMS // seq))
                o_na = _na(naq, nak, nav, lw["bias"], batch=batch, seq=seq)
                x, h = _attn_out(o_mla, o_na, lw["out"][0], lw["out"][1], x, lw["g_post"], lw["g_ffn_pre"],
                                 tile=ffn_tile)
            else:
                if h is None:
                    h = _prenorm(x, lw["g_pre"], tile=tile)
                rec_tile = min(REC_TILE, seq)
                gate, xc, hf = _rec_fwd(h, lw["rec_in"][0], lw["rec_in"][1], *lw["rec_conv"], *lw["dirs"][0],
                                        tile=rec_tile, seq=seq)
                x, h = _rec_bwd(xc, gate, hf, x, *lw["dirs"][1], lw["rec_out"], lw["g_post"], lw["g_ffn_pre"],
                                tile=rec_tile, seq=seq)
            x, h = _ffn(h, x, lw["ffn_up"], lw["ffn_conv_w"], lw["ffn_conv_b"], lw["ffn_down"], lw["g_ffn_post"],
                        lw["g_next"], tile=ffn_tile, fc=min(FFN_CHUNK, lw["ffn_down"].shape[0]), seq=seq,
                        slab_in=li % 2 == 0)
        return x.reshape(batch, seq, d)

    return trunk(x_prompt), trunk(x_sample)
```

```python
import functools
import math

import numpy as np
import jax
import jax.numpy as jnp
from jax import lax
from jax.experimental import pallas as pl
from jax.experimental.pallas import tpu as pltpu

F32 = jnp.float32
BF16 = jnp.bfloat16

EPS = 1e-6
GRID_W = 64
MLA_HEADS = 8
MLA_NOPE = 64
MLA_ROPE = 32
MLA_V = 64
ROPE_BASE = 10000.0
NA_HEADS = 8
NA_HD = 64
NA_KH = 8
NA_KW = 16
LRU_C = 8.0

LANES = 128
SUB = 8
HALO = 16
NEG = -1e30
VMEM_LIMIT = 56 * 1024 * 1024
LOG2E = math.log2(math.e)


def _rms(x, g):
    return x * lax.rsqrt(jnp.mean(x * x, axis=-1, keepdims=True) + EPS) * g


def _gelu_tanh(x):
    c1 = math.sqrt(2.0 / math.pi)
    k1 = -2.0 * c1 * LOG2E
    k2 = k1 * 0.044715
    e = jnp.exp2(x * ((x * x) * k2 + k1))
    return x / (1.0 + e)


def _dot(a, b):
    return jnp.dot(a, b, preferred_element_type=F32)


def _dot_nt(a, b):
    return lax.dot_general(a, b, (((1,), (1,)), ((), ())), preferred_element_type=F32)


def _params(sem):
    return pltpu.CompilerParams(dimension_semantics=sem, vmem_limit_bytes=VMEM_LIMIT)


def _const_spec(shape):
    nd = len(shape)
    return pl.BlockSpec(shape, lambda *_: (0,) * nd)


def _rope(x, cos, sin):
    half = MLA_ROPE // 2
    lane = lax.broadcasted_iota(jnp.int32, x.shape, 1)
    partner = jnp.where(lane < MLA_NOPE + half, pltpu.roll(x, LANES - half, axis=1), pltpu.roll(x, half, axis=1))
    return x * cos + partner * sin


def _attn_in_kernel(x_ref, g_ref, win_ref, qn_ref, wq_ref, kvn_ref, wk_ref, wv_ref,
                    cos_ref, sin_ref,
                    q_ref, k_ref, v_ref, naq_ref, nak_ref, nav_ref, *, q_lora, kv_lora, q_scale, na_scale):
    h = _rms(x_ref[...], g_ref[...]).astype(BF16)
    z = _dot(h, win_ref[...])
    o0 = q_lora
    o1 = o0 + kv_lora
    o3 = o1 + LANES
    na_w = NA_HEADS * NA_HD
    cos = cos_ref[...]
    sin = sin_ref[...]

    nq = _rms(z[:, :o0], qn_ref[...]).astype(BF16)
    q = _dot(nq, wq_ref[...])
    for hd in range(MLA_HEADS):
        sl = slice(hd * LANES, (hd + 1) * LANES)
        q_ref[:, sl] = (_rope(q[:, sl], cos, sin) * q_scale).astype(BF16)

    nkv = _rms(z[:, o0:o1], kvn_ref[...]).astype(BF16)
    k = _dot(nkv, wk_ref[...])
    rot = _rope(z[:, o1:o3], cos, sin)
    for hd in range(MLA_HEADS):
        sl = slice(hd * LANES, (hd + 1) * LANES)
        k_ref[:, sl] = (k[:, sl] + rot).astype(BF16)
    v_ref[...] = _dot(nkv, wv_ref[...]).astype(BF16)

    naq_ref[...] = (z[:, o3:o3 + na_w] * na_scale).astype(BF16)
    nak_ref[...] = z[:, o3 + na_w:o3 + 2 * na_w].astype(BF16)
    nav_ref[...] = z[:, o3 + 2 * na_w:o3 + 3 * na_w].astype(BF16)


def _attn_in(x, g, w_all, qn, wq, kvn, wk, wv, cos_t, sin_t, *, tile, seq):
    n, d = x.shape
    q_lora = qn.shape[1]
    kv_lora = kvn.shape[1]
    hw = MLA_HEADS * LANES
    vw = MLA_HEADS * MLA_V
    na_w = NA_HEADS * NA_HD
    per_seq = seq // tile

    tok = lambda w: pl.BlockSpec((tile, w), lambda i: (i, 0))
    rope = pl.BlockSpec((tile, LANES), lambda i: (lax.rem(i, per_seq), 0))
    kern = functools.partial(_attn_in_kernel, q_lora=q_lora, kv_lora=kv_lora,
                             q_scale=float((MLA_NOPE + MLA_ROPE) ** -0.5 * LOG2E),
                             na_scale=float(NA_HD ** -0.5 * LOG2E))
    return pl.pallas_call(
        kern,
        grid=(n // tile,),
        in_specs=[tok(d), _const_spec(g.shape), _const_spec(w_all.shape), _const_spec(qn.shape),
                  _const_spec(wq.shape), _const_spec(kvn.shape),
                  _const_spec(wk.shape), _const_spec(wv.shape), rope, rope],
        out_specs=[tok(hw), tok(hw), tok(vw), tok(na_w), tok(na_w), tok(na_w)],
        out_shape=[jax.ShapeDtypeStruct((n, hw), BF16), jax.ShapeDtypeStruct((n, hw), BF16),
                   jax.ShapeDtypeStruct((n, vw), BF16),
                   jax.ShapeDtypeStruct((n, na_w), BF16), jax.ShapeDtypeStruct((n, na_w), BF16),
                   jax.ShapeDtypeStruct((n, na_w), BF16)],
        compiler_params=_params(("parallel",)),
        name="attn_in",
    )(x, g, w_all, qn, wq, kvn, wk, wv, cos_t, sin_t)


def _mla_kernel(q_ref, k_ref, v_ref, o_ref):
    tq = q_ref.shape[0]
    seq = k_ref.shape[0]
    heads = [slice(hd * LANES, (hd + 1) * LANES) for hd in range(2)]
    scores = [_dot_nt(q_ref[:, hsl], k_ref[:, hsl]) for hsl in heads]
    probs = [jnp.exp2((s - jnp.max(s, axis=-1, keepdims=True)).astype(BF16)) for s in scores]
    v_ext = jnp.concatenate([v_ref[...], jnp.ones((seq, LANES), BF16)], axis=1)
    accs = [_dot(p, v_ext) for p in probs]
    outs = [acc[:, :LANES] / acc[:, LANES:LANES + 1] for acc in accs]
    lane = lax.broadcasted_iota(jnp.int32, (tq, LANES), 1)
    o_ref[...] = jnp.where(lane < MLA_V, outs[0], outs[1]).astype(BF16)


def _mla(q, k, v, *, batch, seq, tq):
    n = q.shape[0]
    nq = seq // tq
    return pl.pallas_call(
        _mla_kernel,
        grid=(batch, MLA_HEADS // 2, nq),
        in_specs=[pl.BlockSpec((tq, 2 * LANES), lambda b, hp, i: (b * nq + i, hp)),
                  pl.BlockSpec((seq, 2 * LANES), lambda b, hp, i: (b, hp)),
                  pl.BlockSpec((seq, LANES), lambda b, hp, i: (b, hp))],
        out_specs=pl.BlockSpec((tq, LANES), lambda b, hp, i: (b * nq + i, hp)),
        out_shape=jax.ShapeDtypeStruct((n, MLA_HEADS * MLA_V), BF16),
        compiler_params=_params(("parallel", "parallel", "arbitrary")),
        name="mla_attn",
    )(q, k, v)


NA_GROUP = 32


def _na_kernel(q_ref, k_ref, v_ref, bias_ref, o_ref, *, rows):
    win = NA_KH * GRID_W
    lane_q = lax.broadcasted_iota(jnp.int32, (GRID_W, LANES), 1)
    ones = jnp.ones((win, LANES), BF16)
    group = math.gcd(NA_GROUP, rows)

    def body(g, _):
        q0s, k0s, scores = [], [], []
        for j in range(group):
            r = g * group + j
            rs = jnp.clip(r - NA_KH // 2, 0, rows - NA_KH)
            q0 = pl.multiple_of(r * GRID_W, GRID_W)
            k0 = pl.multiple_of(rs * GRID_W, GRID_W)
            qr = q_ref[pl.ds(q0, GRID_W), :]
            zero = jnp.zeros_like(qr)
            q2 = jnp.concatenate([jnp.where(lane_q < NA_HD, qr, zero), jnp.where(lane_q >= NA_HD, qr, zero)],
                                 axis=0)
            scores.append(_dot_nt(q2, k_ref[pl.ds(k0, win), :]) + bias_ref[r - rs])
            q0s.append(q0)
            k0s.append(k0)
        probs = [jnp.exp2((s - jnp.max(s, axis=-1, keepdims=True)).astype(BF16)) for s in scores]
        accs = [_dot(p, jnp.concatenate([v_ref[pl.ds(k0, win), :], ones], axis=1)) for p, k0 in zip(probs, k0s)]
        for q0, acc in zip(q0s, accs):
            o2 = acc[:, :LANES] / acc[:, LANES:LANES + 1]
            o_ref[pl.ds(q0, GRID_W), :] = jnp.where(lane_q < NA_HD, o2[:GRID_W], o2[GRID_W:]).astype(BF16)
        return 0

    lax.fori_loop(0, rows // group, body, 0)


def _na(q, k, v, bias, *, batch, seq):
    n = q.shape[0]
    rows = seq // GRID_W
    blk = pl.BlockSpec((seq, LANES), lambda b, hp: (b, hp))
    return pl.pallas_call(
        functools.partial(_na_kernel, rows=rows),
        grid=(batch, NA_HEADS // 2),
        in_specs=[blk, blk, blk, pl.BlockSpec((None,) + bias.shape[1:], lambda b, hp: (hp, 0, 0, 0))],
        out_specs=blk,
        out_shape=jax.ShapeDtypeStruct((n, NA_HEADS * NA_HD), BF16),
        compiler_params=_params(("parallel", "parallel")),
        name="na_attn",
    )(q, k, v, bias)


def _na_bias_table(rpb):
    n_rel = 2 * NA_KW - 1
    period = 2 * GRID_W
    left = GRID_W - NA_KW
    p = jnp.pad(rpb.astype(F32), ((0, 0), (0, 0), (left, period - n_rel - left)))
    lead = p.shape[:-1]
    flat = jnp.broadcast_to(p[..., None, :], lead + (GRID_W, period)).reshape(lead + (GRID_W * period,))
    skew = flat[..., :GRID_W * (period - 1)].reshape(lead + (GRID_W, period - 1))
    by_col = skew[..., GRID_W - 1:2 * GRID_W - 1]
    cols = np.arange(GRID_W)
    col_start = np.clip(cols - NA_KW // 2, 0, GRID_W - NA_KW)
    kc = np.arange(GRID_W)
    in_win = (kc[None, :] >= col_start[:, None]) & (kc[None, :] < col_start[:, None] + NA_KW)
    by_col = jnp.where(jnp.asarray(in_win)[None, None], by_col * LOG2E, NEG)
    t = jnp.stack([by_col[:, NA_KH - 1 - dl:2 * NA_KH - 1 - dl] for dl in range(NA_KH)], axis=1)
    t = jnp.swapaxes(t, 2, 3).reshape(rpb.shape[0] // 2, 2, NA_KH, GRID_W, NA_KH * GRID_W)
    return jnp.swapaxes(t, 1, 2).reshape(rpb.shape[0] // 2, NA_KH, 2 * GRID_W, NA_KH * GRID_W)


def _to_slabs(val, nat_ref):
    rows = val.shape[0]
    s = rows // SUB
    ng = nat_ref.shape[0]
    for k in range(ng):
        nat_ref[k] = val[:, k * LANES:(k + 1) * LANES].astype(F32)
    return jnp.concatenate(
        [jnp.concatenate([nat_ref[k, pl.ds(r, s, stride=SUB), :] for k in range(ng)], axis=1) for r in range(SUB)],
        axis=0)


def _attn_out_kernel(om_ref, on_ref, wm_ref, wn_ref, x_ref, gpost_ref, gnext_ref, xo_ref, ho_ref, nat_ref):
    m = _dot(om_ref[...], wm_ref[...]) + _dot(on_ref[...], wn_ref[...])
    xo = x_ref[...] + _rms(m, gpost_ref[...])
    xo_ref[...] = xo
    ho_ref[...] = _to_slabs(_rms(xo, gnext_ref[...]), nat_ref).astype(BF16)


def _attn_out(o_mla, o_na, w_m, w_n, x, g_post, g_next, *, tile):
    n, d = x.shape
    tok = lambda w: pl.BlockSpec((tile, w), lambda i: (i, 0))
    return pl.pallas_call(
        _attn_out_kernel,
        grid=(n // tile,),
        in_specs=[tok(o_mla.shape[1]), tok(o_na.shape[1]), _const_spec(w_m.shape), _const_spec(w_n.shape),
                  tok(d), _const_spec(g_post.shape), _const_spec(g_next.shape)],
        out_specs=[tok(d), tok(d)],
        out_shape=[jax.ShapeDtypeStruct((n, d), F32), jax.ShapeDtypeStruct((n, d), BF16)],
        scratch_shapes=[pltpu.VMEM((d // LANES, tile, LANES), F32)],
        compiler_params=_params(("parallel",)),
        name="attn_out",
    )(o_mla, o_na, w_m, w_n, x, g_post, g_next)


def _halo_specs(tile, d, n):
    per = tile // HALO
    last = n // HALO - 1
    return [pl.BlockSpec((HALO, d), lambda i, *_: (jnp.maximum(i * per - 1, 0), 0)),
            pl.BlockSpec((tile, d), lambda i, *_: (i, 0)),
            pl.BlockSpec((HALO, d), lambda i, *_: (jnp.minimum((i + 1) * per, last), 0))]


def _fill_halo(hh_ref, hp_ref, hm_ref, hn_ref, pos0, tile, seq):
    zero = jnp.zeros(hp_ref.shape, hp_ref.dtype)
    hh_ref[0:HALO, :] = jnp.where(pos0 == 0, zero, hp_ref[...])
    hh_ref[HALO:HALO + tile, :] = hm_ref[...]
    hh_ref[HALO + tile:, :] = jnp.where(pos0 + tile == seq, zero, hn_ref[...])


def _dwconv(u, w, b, tile, left):
    taps = w.shape[0]
    n = u.shape[0]
    shifted = {0: u}
    for d in range(1, taps - left):
        shifted[d] = pltpu.roll(shifted[d - 1], n - 1, axis=0)
    for d in range(1, left + 1):
        shifted[-d] = pltpu.roll(shifted[1 - d], 1, axis=0)
    y = b
    for kk in range(taps):
        y = y + w[kk:kk + 1, :] * shifted[kk - left][HALO:HALO + tile, :]
    return y


def _slab_conv3(u, w, b, tile):
    s = tile // SUB
    slabs = [u[r * s:(r + 1) * s, :] for r in range(SUB)]
    row = lax.broadcasted_iota(jnp.int32, slabs[0].shape, 0)
    before = jnp.where(row == 0, u[tile:tile + 1, :], pltpu.roll(slabs[SUB - 1], 1, axis=0))
    after = jnp.where(row == s - 1, u[tile + 1:tile + 2, :], pltpu.roll(slabs[0], s - 1, axis=0))
    out = []
    for r in range(SUB):
        left = slabs[r - 1] if r > 0 else before
        right = slabs[r + 1] if r < SUB - 1 else after
        out.append(b + w[0:1, :] * left + w[1:2, :] * slabs[r] + w[2:3, :] * right)
    return jnp.concatenate(out, axis=0)


def _ffn_kernel(*refs, tile, seq, emit_next, slab_in):
    if emit_next:
        (hp_ref, hm_ref, hn_ref, x_ref, wg_ref, wv_ref, cwg_ref, cwv_ref, cbg_ref, cbv_ref, wd_ref,
         gpost_ref, gnext_ref, xo_ref, ho_ref, hh_ref, acc_ref, nat_ref) = refs
    else:
        (hp_ref, hm_ref, hn_ref, x_ref, wg_ref, wv_ref, cwg_ref, cwv_ref, cbg_ref, cbv_ref, wd_ref,
         gpost_ref, xo_ref, hh_ref, acc_ref, nat_ref) = refs
    i = pl.program_id(0)
    c = pl.program_id(1)
    s = tile // SUB
    ng = nat_ref.shape[0]

    @pl.when(c == 0)
    def _():
        pos0 = lax.rem(i * tile, seq)
        if slab_in:
            hh_ref[0:tile, :] = hm_ref[...]
        else:
            hh_ref[0:tile, :] = _to_slabs(hm_ref[...], nat_ref).astype(BF16)
        zero = jnp.zeros((1, hm_ref.shape[1]), F32)
        prev = jnp.where(pos0 == 0, zero, hp_ref[HALO - 1:HALO, :].astype(F32))
        nxt = jnp.where(pos0 + tile == seq, zero, hn_ref[0:1, :].astype(F32))
        tail = jnp.concatenate([prev, nxt, jnp.zeros((HALO - 2, hm_ref.shape[1]), F32)], axis=0)
        hh_ref[tile:, :] = tail.astype(BF16)
        acc_ref[...] = jnp.zeros(acc_ref.shape, F32)

    hh = hh_ref[...]
    g = _slab_conv3(_dot(hh, wg_ref[...]), cwg_ref[...], cbg_ref[...], tile)
    val = _slab_conv3(_dot(hh, wv_ref[...]), cwv_ref[...], cbv_ref[...], tile)
    a = (_gelu_tanh(g) * val).astype(BF16)
    acc_ref[...] += _dot(a, wd_ref[...])

    @pl.when(c == pl.num_programs(1) - 1)
    def _():
        f = _rms(acc_ref[...], gpost_ref[...])
        for r in range(SUB):
            for k in range(ng):
                nat_ref[k, pl.ds(r, s, stride=SUB), :] = f[r * s:(r + 1) * s, k * LANES:(k + 1) * LANES]
        xo = x_ref[...] + jnp.concatenate([nat_ref[k] for k in range(ng)], axis=1)
        xo_ref[...] = xo
        if emit_next:
            ho_ref[...] = _rms(xo, gnext_ref[...]).astype(BF16)


def _ffn(h, x, w_up, conv_w, conv_b, w_down, g_post, g_next, *, tile, fc, seq, slab_in):
    n, d = x.shape
    d_ff = w_down.shape[0]
    nc = d_ff // fc
    kk = conv_w.shape[0]
    assert kk == 3 and d % LANES == 0 and tile % (SUB * SUB) == 0
    emit_next = g_next is not None
    tok = pl.BlockSpec((tile, d), lambda i, c: (i, 0))
    in_specs = _halo_specs(tile, d, n) + [
        tok,
        pl.BlockSpec((d, fc), lambda i, c: (0, c)),
        pl.BlockSpec((d, fc), lambda i, c: (0, nc + c)),
        pl.BlockSpec((kk, fc), lambda i, c: (0, c)),
        pl.BlockSpec((kk, fc), lambda i, c: (0, nc + c)),
        pl.BlockSpec((1, fc), lambda i, c: (0, c)),
        pl.BlockSpec((1, fc), lambda i, c: (0, nc + c)),
        pl.BlockSpec((fc, d), lambda i, c: (c, 0)),
        _const_spec(g_post.shape)]
    args = [h, h, h, x, w_up, w_up, conv_w, conv_w, conv_b, conv_b, w_down, g_post]
    out_specs = [tok]
    out_shape = [jax.ShapeDtypeStruct((n, d), F32)]
    if emit_next:
        in_specs.append(_const_spec(g_next.shape))
        args.append(g_next)
        out_specs.append(tok)
        out_shape.append(jax.ShapeDtypeStruct((n, d), BF16))
    res = pl.pallas_call(
        functools.partial(_ffn_kernel, tile=tile, seq=seq, emit_next=emit_next, slab_in=slab_in),
        grid=(n // tile, nc),
        in_specs=in_specs,
        out_specs=out_specs,
        out_shape=out_shape,
        scratch_shapes=[pltpu.VMEM((tile + HALO, d), BF16), pltpu.VMEM((tile, d), F32),
                        pltpu.VMEM((d // LANES, tile, LANES), F32)],
        compiler_params=_params(("parallel", "arbitrary")),
        name="conv_ffn",
    )(*args)
    return (res[0], res[1]) if emit_next else (res[0], None)


def _lru_coeffs(xc, xc_bf, gw_ref, gab_ref, gxb_ref, lam_ref):
    nb, bw, _ = gw_ref.shape
    neg = -lam_ref[...]
    sp = jnp.maximum(neg, 0.0) + jnp.log1p(jnp.exp(-jnp.abs(neg)))
    half_c = (-0.5 * LRU_C) * sp
    half_gab = 0.5 * gab_ref[...]
    half_gxb = 0.5 * gxb_ref[...]
    a_parts, b_parts = [], []
    for blk in range(nb):
        sl = slice(blk * bw, (blk + 1) * bw)
        rg = _dot(xc_bf[:, sl], gw_ref[blk])
        hc = half_c[:, sl]
        log_a = hc * jnp.tanh(rg[:, :bw] + half_gab[:, sl]) + hc
        hx = 0.5 * xc[:, sl]
        gated_x = hx * jnp.tanh(rg[:, bw:] + half_gxb[:, sl]) + hx
        t = jnp.tanh(log_a)
        one_minus_a2 = (-2.0 * t) / (1.0 - t)
        root = jnp.where(one_minus_a2 > 0.0, one_minus_a2 * lax.rsqrt(one_minus_a2), 0.0)
        a_parts.append(jnp.exp2(log_a * LOG2E))
        b_parts.append(root * gated_x)
    return jnp.concatenate(a_parts, axis=1), jnp.concatenate(b_parts, axis=1)


def _scan_tile(a_ref, b_ref, carry_ref, *, reverse):
    t, ch = a_ref.shape
    n_chunks = t // SUB
    row = lax.broadcasted_iota(jnp.int32, (SUB, ch), 0)

    def body(j, carry):
        jj = (n_chunks - 1 - j) if reverse else j
        r0 = pl.multiple_of(jj * SUB, SUB)
        a = a_ref[pl.ds(r0, SUB), :]
        b = b_ref[pl.ds(r0, SUB), :]
        for dist in (1, 2, 4):
            shift = (SUB - dist) if reverse else dist
            a_sh = pltpu.roll(a, shift, axis=0)
            b_sh = pltpu.roll(b, shift, axis=0)
            valid = (row < SUB - dist) if reverse else (row >= dist)
            b = jnp.where(valid, a * b_sh + b, b)
            a = jnp.where(valid, a * a_sh, a)
        hcur = a * carry + b
        b_ref[pl.ds(r0, SUB), :] = hcur
        edge = hcur[0:1, :] if reverse else hcur[SUB - 1:SUB, :]
        return jnp.broadcast_to(edge, (SUB, ch))

    carry_ref[...] = lax.fori_loop(0, n_chunks, body, carry_ref[...], unroll=4)


def _rec_fwd_kernel(hp_ref, hm_ref, hn_ref, wg_ref, wx_ref, cw_ref, cb_ref, gw_ref, gab_ref, gxb_ref,
                    lam_ref, gate_ref, xc_ref, hf_ref, hh_ref, a_ref, b_ref, carry_ref, *, tile, seq):
    pos0 = lax.rem(pl.program_id(0) * tile, seq)
    _fill_halo(hh_ref, hp_ref, hm_ref, hn_ref, pos0, tile, seq)

    @pl.when(pos0 == 0)
    def _():
        carry_ref[...] = jnp.zeros(carry_ref.shape, F32)

    gate_ref[...] = _dot(hm_ref[...], wg_ref[...]).astype(BF16)
    left = (cw_ref.shape[0] - 1) // 2
    xc = _dwconv(_dot(hh_ref[...], wx_ref[...]), cw_ref[...], cb_ref[...], tile, left)
    xc_bf = xc.astype(BF16)
    xc_ref[...] = xc_bf
    a, b = _lru_coeffs(xc, xc_bf, gw_ref, gab_ref, gxb_ref, lam_ref)
    a_ref[...] = a
    b_ref[...] = b
    _scan_tile(a_ref, b_ref, carry_ref, reverse=False)
    hf_ref[...] = b_ref[...].astype(BF16)


def _rec_fwd(h, w_g, w_x, conv_w, conv_b, gw, gab, gxb, lam, *, tile, seq):
    n, d = h.shape
    d_rnn = w_x.shape[1]
    tok = pl.BlockSpec((tile, d_rnn), lambda i: (i, 0))
    in_specs = _halo_specs(tile, d, n) + [_const_spec(a.shape) for a in
                                          (w_g, w_x, conv_w, conv_b, gw, gab, gxb, lam)]
    out = jax.ShapeDtypeStruct((n, d_rnn), BF16)
    return pl.pallas_call(
        functools.partial(_rec_fwd_kernel, tile=tile, seq=seq),
        grid=(n // tile,),
        in_specs=in_specs,
        out_specs=[tok, tok, tok],
        out_shape=[out, out, out],
        scratch_shapes=[pltpu.VMEM((tile + 2 * HALO, d), BF16), pltpu.VMEM((tile, d_rnn), F32),
                        pltpu.VMEM((tile, d_rnn), F32), pltpu.VMEM((SUB, d_rnn), F32)],
        compiler_params=_params(("arbitrary",)),
        name="rglru_fwd",
    )(h, h, h, w_g, w_x, conv_w, conv_b, gw, gab, gxb, lam)


def _rec_bwd_kernel(xc_ref, gate_ref, hf_ref, x_ref, gw_ref, gab_ref, gxb_ref, lam_ref, wo_ref,
                    gpost_ref, gnext_ref, xo_ref, ho_ref, a_ref, b_ref, carry_ref, *, tile, seq):
    i = pl.num_programs(0) - 1 - pl.program_id(0)
    pos0 = lax.rem(i * tile, seq)

    @pl.when(pos0 + tile == seq)
    def _():
        carry_ref[...] = jnp.zeros(carry_ref.shape, F32)

    xc_bf = xc_ref[...]
    a, b = _lru_coeffs(xc_bf.astype(F32), xc_bf, gw_ref, gab_ref, gxb_ref, lam_ref)
    a_ref[...] = a
    b_ref[...] = b
    _scan_tile(a_ref, b_ref, carry_ref, reverse=True)
    y = (hf_ref[...].astype(F32) + b_ref[...]) * _gelu_tanh(gate_ref[...].astype(F32))
    m = _dot(y.astype(BF16), wo_ref[...])
    xo = x_ref[...] + _rms(m, gpost_ref[...])
    xo_ref[...] = xo
    ho_ref[...] = _rms(xo, gnext_ref[...]).astype(BF16)


def _rec_bwd(xc, gate, hf, x, gw, gab, gxb, lam, w_out, g_post, g_next, *, tile, seq):
    n, d = x.shape
    d_rnn = xc.shape[1]
    nt = n // tile
    rtok = lambda w: pl.BlockSpec((tile, w), lambda i: (nt - 1 - i, 0))
    in_specs = [rtok(d_rnn), rtok(d_rnn), rtok(d_rnn), rtok(d)] + [
        _const_spec(a.shape) for a in (gw, gab, gxb, lam, w_out, g_post, g_next)]
    return pl.pallas_call(
        functools.partial(_rec_bwd_kernel, tile=tile, seq=seq),
        grid=(nt,),
        in_specs=in_specs,
        out_specs=[rtok(d), rtok(d)],
        out_shape=[jax.ShapeDtypeStruct((n, d), F32), jax.ShapeDtypeStruct((n, d), BF16)],
        scratch_shapes=[pltpu.VMEM((tile, d_rnn), F32), pltpu.VMEM((tile, d_rnn), F32),
                        pltpu.VMEM((SUB, d_rnn), F32)],
        compiler_params=_params(("arbitrary",)),
        name="rglru_bwd",
    )(xc, gate, hf, x, gw, gab, gxb, lam, w_out, g_post, g_next)


def _row(v):
    return v.reshape(1, -1).astype(F32)


def _rope_tables(s_max):
    pos = jnp.arange(s_max, dtype=F32)
    inv = ROPE_BASE ** (-jnp.arange(0, MLA_ROPE, 2, dtype=F32) / MLA_ROPE)
    ang = pos[:, None] * inv[None, :]
    cos, sin = jnp.cos(ang), jnp.sin(ang)
    pad = LANES - MLA_NOPE - MLA_ROPE
    cos_t = jnp.concatenate([jnp.ones((s_max, MLA_NOPE), F32), cos, cos, jnp.zeros((s_max, pad), F32)], axis=1)
    sin_t = jnp.concatenate([jnp.zeros((s_max, MLA_NOPE), F32), -sin, sin, jnp.zeros((s_max, pad), F32)], axis=1)
    return cos_t, sin_t


def _attn_weights(w_in, w_uq, w_ukv):
    d = w_in.shape[0]
    q_lora = w_uq.shape[0]
    kv_lora = w_ukv.shape[0]
    o1 = q_lora + kv_lora
    o2 = o1 + MLA_ROPE
    pad = LANES - MLA_NOPE - MLA_ROPE
    kpe = w_in[:, o1:o2]
    zl = jnp.zeros((d, MLA_NOPE), w_in.dtype)
    zr = jnp.zeros((d, pad), w_in.dtype)
    w_all = jnp.concatenate([w_in[:, :o1], zl, kpe, zr, w_in[:, o2:]], axis=1)

    wq = w_uq.reshape(q_lora, MLA_HEADS, MLA_NOPE + MLA_ROPE)
    zq = jnp.zeros((q_lora, MLA_HEADS, pad), w_uq.dtype)
    wq_p = jnp.concatenate([wq, zq], axis=-1).reshape(q_lora, MLA_HEADS * LANES)

    wkv = w_ukv.reshape(kv_lora, MLA_HEADS, MLA_NOPE + MLA_V)
    zk = jnp.zeros((kv_lora, MLA_HEADS, LANES - MLA_NOPE), w_ukv.dtype)
    wk_p = jnp.concatenate([wkv[..., :MLA_NOPE], zk], axis=-1).reshape(kv_lora, MLA_HEADS * LANES)
    wv = wkv[..., MLA_NOPE:].reshape(kv_lora, MLA_HEADS * MLA_V)
    return tuple(a.astype(BF16) for a in (w_all, wq_p, wk_p, wv))


TOKEN_TILE = 1024
FFN_TILE = 1024
FFN_CHUNK = 1024
REC_TILE = 1024
MLA_SCORE_ELEMS = 2 ** 21


def kernel(x_prompt, x_sample, norm_mix_pre, norm_mix_post, norm_ffn_pre, norm_ffn_post, w_in_attn, q_norm, w_uq, kv_norm, w_ukv, na_rpb, w_out_attn, w_in_rec, conv_w_rec, conv_b_rec, gate_a_w, gate_a_b, gate_x_w, gate_x_b, lru_lambda, w_out_rec, w_ffn_up, conv_w_ffn, conv_b_ffn, w_ffn_down):
    depth = norm_mix_pre.shape[0]
    d_rnn = w_out_rec.shape[1]
    n_mla = MLA_HEADS * MLA_V
    cos_t, sin_t = _rope_tables(max(x_prompt.shape[1], x_sample.shape[1]))

    layers = []
    for li in range(depth):
        j = li // 2
        lw = dict(ffn_up=w_ffn_up[li].astype(BF16), ffn_down=w_ffn_down[li].astype(BF16),
                  ffn_conv_w=conv_w_ffn[li], ffn_conv_b=_row(conv_b_ffn[li]),
                  g_pre=_row(norm_mix_pre[li]), g_post=_row(norm_mix_post[li]),
                  g_ffn_pre=_row(norm_ffn_pre[li]), g_ffn_post=_row(norm_ffn_post[li]),
                  g_next=_row(norm_mix_pre[li + 1]) if li + 1 < depth else None)
        if li % 2 == 0:
            lw["attn"] = _attn_weights(w_in_attn[j], w_uq[j], w_ukv[j])
            lw["q_norm"] = _row(q_norm[j])
            lw["kv_norm"] = _row(kv_norm[j])
            lw["bias"] = _na_bias_table(na_rpb[j])
            w_o = w_out_attn[j].astype(BF16)
            lw["out"] = (w_o[:n_mla], w_o[n_mla:])
        else:
            w_in = w_in_rec[j].astype(BF16)
            lw["rec_in"] = (w_in[:, :d_rnn], w_in[:, d_rnn:])
            lw["rec_conv"] = (conv_w_rec[j], _row(conv_b_rec[j]))
            lw["dirs"] = [((0.5 * jnp.concatenate([gate_a_w[j, e], gate_x_w[j, e]], axis=-1)).astype(BF16),
                           _row(gate_a_b[j, e]), _row(gate_x_b[j, e]), _row(lru_lambda[j, e])) for e in range(2)]
            lw["rec_out"] = w_out_rec[j].astype(BF16)
        layers.append(lw)

    def trunk(x3):
        batch, seq, d = x3.shape
        x = x3.reshape(batch * seq, d)
        tile = min(TOKEN_TILE, seq)
        ffn_tile = min(FFN_TILE, seq)
        h = None
        for li, lw in enumerate(layers):
            if li % 2 == 0:
                w_all, wq_p, wk_p, wv = lw["attn"]
                q, k, v, naq, nak, nav = _attn_in(x, lw["g_pre"], w_all, lw["q_norm"], wq_p, lw["kv_norm"], wk_p, wv,
                                                  cos_t, sin_t, tile=tile, seq=seq)
                o_mla = _mla(q, k, v, batch=batch, seq=seq, tq=min(seq, MLA_SCORE_ELEMS // seq))
                o_na = _na(naq, nak, nav, lw["bias"], batch=batch, seq=seq)
                x, h = _attn_out(o_mla, o_na, lw["out"][0], lw["out"][1], x, lw["g_post"], lw["g_ffn_pre"],
                                 tile=ffn_tile)
            else:
                rec_tile = min(REC_TILE, seq)
                gate, xc, hf = _rec_fwd(h, lw["rec_in"][0], lw["rec_in"][1], *lw["rec_conv"], *lw["dirs"][0],
                                        tile=rec_tile, seq=seq)
                x, h = _rec_bwd(xc, gate, hf, x, *lw["dirs"][1], lw["rec_out"], lw["g_post"], lw["g_ffn_pre"],
                                tile=rec_tile, seq=seq)
            x, h = _ffn(h, x, lw["ffn_up"], lw["ffn_conv_w"], lw["ffn_conv_b"], lw["ffn_down"], lw["g_ffn_post"],
                        lw["g_next"], tile=ffn_tile, fc=min(FFN_CHUNK, lw["ffn_down"].shape[0]), seq=seq,
                        slab_in=li % 2 == 0)
        return x.reshape(batch, seq, d)

    return trunk(x_prompt), trunk(x_sample)
```

```python
import functools
import math

import numpy as np
import jax
import jax.numpy as jnp
from jax import lax
from jax.experimental import pallas as pl
from jax.experimental.pallas import tpu as pltpu

F32 = jnp.float32
BF16 = jnp.bfloat16

EPS = 1e-6
GRID_W = 64
MLA_HEADS = 8
MLA_NOPE = 64
MLA_ROPE = 32
MLA_V = 64
ROPE_BASE = 10000.0
NA_HEADS = 8
NA_HD = 64
NA_KH = 8
NA_KW = 16
LRU_C = 8.0

LANES = 128
SUB = 8
HALO = 16
NEG = -1e30
VMEM_LIMIT = 56 * 1024 * 1024
LOG2E = math.log2(math.e)


def _rms(x, g):
    return x * lax.rsqrt(jnp.mean(x * x, axis=-1, keepdims=True) + EPS) * g


def _gelu_tanh(x):
    c1 = math.sqrt(2.0 / math.pi)
    k1 = -2.0 * c1 * LOG2E
    k2 = k1 * 0.044715
    e = jnp.exp2(x * ((x * x) * k2 + k1))
    return x / (1.0 + e)


def _dot(a, b):
    return jnp.dot(a, b, preferred_element_type=F32)


def _dot_nt(a, b):
    return lax.dot_general(a, b, (((1,), (1,)), ((), ())), preferred_element_type=F32)


def _params(sem):
    return pltpu.CompilerParams(dimension_semantics=sem, vmem_limit_bytes=VMEM_LIMIT)


def _const_spec(shape):
    nd = len(shape)
    return pl.BlockSpec(shape, lambda *_: (0,) * nd)


def _rope(x, cos, sin):
    half = MLA_ROPE // 2
    lane = lax.broadcasted_iota(jnp.int32, x.shape, 1)
    partner = jnp.where(lane < MLA_NOPE + half, pltpu.roll(x, LANES - half, axis=1), pltpu.roll(x, half, axis=1))
    return x * cos + partner * sin


def _attn_in_kernel(x_ref, g_ref, win_ref, qn_ref, wq_ref, kvn_ref, wk_ref, wv_ref,
                    cos_ref, sin_ref,
                    q_ref, k_ref, v_ref, naq_ref, nak_ref, nav_ref, *, q_lora, kv_lora, q_scale, na_scale):
    h = _rms(x_ref[...], g_ref[...]).astype(BF16)
    z = _dot(h, win_ref[...])
    o0 = q_lora
    o1 = o0 + kv_lora
    o3 = o1 + LANES
    na_w = NA_HEADS * NA_HD
    cos = cos_ref[...]
    sin = sin_ref[...]

    nq = _rms(z[:, :o0], qn_ref[...]).astype(BF16)
    q = _dot(nq, wq_ref[...])
    for hd in range(MLA_HEADS):
        sl = slice(hd * LANES, (hd + 1) * LANES)
        q_ref[:, sl] = (_rope(q[:, sl], cos, sin) * q_scale).astype(BF16)

    nkv = _rms(z[:, o0:o1], kvn_ref[...]).astype(BF16)
    k = _dot(nkv, wk_ref[...])
    rot = _rope(z[:, o1:o3], cos, sin)
    for hd in range(MLA_HEADS):
        sl = slice(hd * LANES, (hd + 1) * LANES)
        k_ref[:, sl] = (k[:, sl] + rot).astype(BF16)
    v_ref[...] = _dot(nkv, wv_ref[...]).astype(BF16)

    naq_ref[...] = (z[:, o3:o3 + na_w] * na_scale).astype(BF16)
    nak_ref[...] = z[:, o3 + na_w:o3 + 2 * na_w].astype(BF16)
    nav_ref[...] = z[:, o3 + 2 * na_w:o3 + 3 * na_w].astype(BF16)


def _attn_in(x, g, w_all, qn, wq, kvn, wk, wv, cos_t, sin_t, *, tile, seq):
    n, d = x.shape
    q_lora = qn.shape[1]
    kv_lora = kvn.shape[1]
    hw = MLA_HEADS * LANES
    vw = MLA_HEADS * MLA_V
    na_w = NA_HEADS * NA_HD
    per_seq = seq // tile

    tok = lambda w: pl.BlockSpec((tile, w), lambda i: (i, 0))
    rope = pl.BlockSpec((tile, LANES), lambda i: (lax.rem(i, per_seq), 0))
    kern = functools.partial(_attn_in_kernel, q_lora=q_lora, kv_lora=kv_lora,
                             q_scale=float((MLA_NOPE + MLA_ROPE) ** -0.5 * LOG2E),
                             na_scale=float(NA_HD ** -0.5 * LOG2E))
    return pl.pallas_call(
        kern,
        grid=(n // tile,),
        in_specs=[tok(d), _const_spec(g.shape), _const_spec(w_all.shape), _const_spec(qn.shape),
                  _const_spec(wq.shape), _const_spec(kvn.shape),
                  _const_spec(wk.shape), _const_spec(wv.shape), rope, rope],
        out_specs=[tok(hw), tok(hw), tok(vw), tok(na_w), tok(na_w), tok(na_w)],
        out_shape=[jax.ShapeDtypeStruct((n, hw), BF16), jax.ShapeDtypeStruct((n, hw), BF16),
                   jax.ShapeDtypeStruct((n, vw), BF16),
                   jax.ShapeDtypeStruct((n, na_w), BF16), jax.ShapeDtypeStruct((n, na_w), BF16),
                   jax.ShapeDtypeStruct((n, na_w), BF16)],
        compiler_params=_params(("parallel",)),
        name="attn_in",
    )(x, g, w_all, qn, wq, kvn, wk, wv, cos_t, sin_t)


def _mla_kernel(q_ref, k_ref, v_ref, o_ref):
    tq = q_ref.shape[0]
    seq = k_ref.shape[0]
    heads = [slice(hd * LANES, (hd + 1) * LANES) for hd in range(2)]
    scores = [_dot_nt(q_ref[:, hsl], k_ref[:, hsl]) for hsl in heads]
    probs = [jnp.exp2((s - jnp.max(s, axis=-1, keepdims=True)).astype(BF16)) for s in scores]
    v_ext = jnp.concatenate([v_ref[...], jnp.ones((seq, LANES), BF16)], axis=1)
    accs = [_dot(p, v_ext) for p in probs]
    outs = [acc[:, :LANES] / acc[:, LANES:LANES + 1] for acc in accs]
    lane = lax.broadcasted_iota(jnp.int32, (tq, LANES), 1)
    o_ref[...] = jnp.where(lane < MLA_V, outs[0], outs[1]).astype(BF16)


def _mla(q, k, v, *, batch, seq, tq):
    n = q.shape[0]
    nq = seq // tq
    return pl.pallas_call(
        _mla_kernel,
        grid=(batch, MLA_HEADS // 2, nq),
        in_specs=[pl.BlockSpec((tq, 2 * LANES), lambda b, hp, i: (b * nq + i, hp)),
                  pl.BlockSpec((seq, 2 * LANES), lambda b, hp, i: (b, hp)),
                  pl.BlockSpec((seq, LANES), lambda b, hp, i: (b, hp))],
        out_specs=pl.BlockSpec((tq, LANES), lambda b, hp, i: (b * nq + i, hp)),
        out_shape=jax.ShapeDtypeStruct((n, MLA_HEADS * MLA_V), BF16),
        compiler_params=_params(("parallel", "parallel", "arbitrary")),
        name="mla_attn",
    )(q, k, v)


NA_GROUP = 32


def _na_kernel(q_ref, k_ref, v_ref, bias_ref, o_ref, *, rows):
    win = NA_KH * GRID_W
    lane_q = lax.broadcasted_iota(jnp.int32, (GRID_W, LANES), 1)
    ones = jnp.ones((win, LANES), BF16)
    group = math.gcd(NA_GROUP, rows)

    def body(g, _):
        q0s, k0s, scores = [], [], []
        for j in range(group):
            r = g * group + j
            rs = jnp.clip(r - NA_KH // 2, 0, rows - NA_KH)
            q0 = pl.multiple_of(r * GRID_W, GRID_W)
            k0 = pl.multiple_of(rs * GRID_W, GRID_W)
            qr = q_ref[pl.ds(q0, GRID_W), :]
            zero = jnp.zeros_like(qr)
            q2 = jnp.concatenate([jnp.where(lane_q < NA_HD, qr, zero), jnp.where(lane_q >= NA_HD, qr, zero)],
                                 axis=0)
            scores.append(_dot_nt(q2, k_ref[pl.ds(k0, win), :]) + bias_ref[r - rs])
            q0s.append(q0)
            k0s.append(k0)
        probs = [jnp.exp2((s - jnp.max(s, axis=-1, keepdims=True)).astype(BF16)) for s in scores]
        accs = [_dot(p, jnp.concatenate([v_ref[pl.ds(k0, win), :], ones], axis=1)) for p, k0 in zip(probs, k0s)]
        for q0, acc in zip(q0s, accs):
            o2 = acc[:, :LANES] / acc[:, LANES:LANES + 1]
            o_ref[pl.ds(q0, GRID_W), :] = jnp.where(lane_q < NA_HD, o2[:GRID_W], o2[GRID_W:]).astype(BF16)
        return 0

    lax.fori_loop(0, rows // group, body, 0)


def _na(q, k, v, bias, *, batch, seq):
    n = q.shape[0]
    rows = seq // GRID_W
    blk = pl.BlockSpec((seq, LANES), lambda b, hp: (b, hp))
    return pl.pallas_call(
        functools.partial(_na_kernel, rows=rows),
        grid=(batch, NA_HEADS // 2),
        in_specs=[blk, blk, blk, pl.BlockSpec((None,) + bias.shape[1:], lambda b, hp: (hp, 0, 0, 0))],
        out_specs=blk,
        out_shape=jax.ShapeDtypeStruct((n, NA_HEADS * NA_HD), BF16),
        compiler_params=_params(("parallel", "parallel")),
        name="na_attn",
    )(q, k, v, bias)


def _na_bias_table(rpb):
    n_rel = 2 * NA_KW - 1
    period = 2 * GRID_W
    left = GRID_W - NA_KW
    p = jnp.pad(rpb.astype(F32), ((0, 0), (0, 0), (left, period - n_rel - left)))
    lead = p.shape[:-1]
    flat = jnp.broadcast_to(p[..., None, :], lead + (GRID_W, period)).reshape(lead + (GRID_W * period,))
    skew = flat[..., :GRID_W * (period - 1)].reshape(lead + (GRID_W, period - 1))
    by_col = skew[..., GRID_W - 1:2 * GRID_W - 1]
    cols = np.arange(GRID_W)
    col_start = np.clip(cols - NA_KW // 2, 0, GRID_W - NA_KW)
    kc = np.arange(GRID_W)
    in_win = (kc[None, :] >= col_start[:, None]) & (kc[None, :] < col_start[:, None] + NA_KW)
    by_col = jnp.where(jnp.asarray(in_win)[None, None], by_col * LOG2E, NEG)
    t = jnp.stack([by_col[:, NA_KH - 1 - dl:2 * NA_KH - 1 - dl] for dl in range(NA_KH)], axis=1)
    t = jnp.swapaxes(t, 2, 3).reshape(rpb.shape[0] // 2, 2, NA_KH, GRID_W, NA_KH * GRID_W)
    return jnp.swapaxes(t, 1, 2).reshape(rpb.shape[0] // 2, NA_KH, 2 * GRID_W, NA_KH * GRID_W)


def _to_slabs(val, nat_ref):
    rows = val.shape[0]
    s = rows // SUB
    ng = nat_ref.shape[0]
    for k in range(ng):
        nat_ref[k] = val[:, k * LANES:(k + 1) * LANES].astype(F32)
    return jnp.concatenate(
        [jnp.concatenate([nat_ref[k, pl.ds(r, s, stride=SUB), :] for k in range(ng)], axis=1) for r in range(SUB)],
        axis=0)


def _attn_out_kernel(om_ref, on_ref, wm_ref, wn_ref, x_ref, gpost_ref, gnext_ref, xo_ref, ho_ref, nat_ref):
    m = _dot(om_ref[...], wm_ref[...]) + _dot(on_ref[...], wn_ref[...])
    xo = x_ref[...] + _rms(m, gpost_ref[...])
    xo_ref[...] = xo
    ho_ref[...] = _to_slabs(_rms(xo, gnext_ref[...]), nat_ref).astype(BF16)


def _attn_out(o_mla, o_na, w_m, w_n, x, g_post, g_next, *, tile):
    n, d = x.shape
    tok = lambda w: pl.BlockSpec((tile, w), lambda i: (i, 0))
    return pl.pallas_call(
        _attn_out_kernel,
        grid=(n // tile,),
        in_specs=[tok(o_mla.shape[1]), tok(o_na.shape[1]), _const_spec(w_m.shape), _const_spec(w_n.shape),
                  tok(d), _const_spec(g_post.shape), _const_spec(g_next.shape)],
        out_specs=[tok(d), tok(d)],
        out_shape=[jax.ShapeDtypeStruct((n, d), F32), jax.ShapeDtypeStruct((n, d), BF16)],
        scratch_shapes=[pltpu.VMEM((d // LANES, tile, LANES), F32)],
        compiler_params=_params(("parallel",)),
        name="attn_out",
    )(o_mla, o_na, w_m, w_n, x, g_post, g_next)


def _halo_specs(tile, d, n):
    per = tile // HALO
    last = n // HALO - 1
    return [pl.BlockSpec((HALO, d), lambda i, *_: (jnp.maximum(i * per - 1, 0), 0)),
            pl.BlockSpec((tile, d), lambda i, *_: (i, 0)),
            pl.BlockSpec((HALO, d), lambda i, *_: (jnp.minimum((i + 1) * per, last), 0))]


def _fill_halo(hh_ref, hp_ref, hm_ref, hn_ref, pos0, tile, seq):
    zero = jnp.zeros(hp_ref.shape, hp_ref.dtype)
    hh_ref[0:HALO, :] = jnp.where(pos0 == 0, zero, hp_ref[...])
    hh_ref[HALO:HALO + tile, :] = hm_ref[...]
    hh_ref[HALO + tile:, :] = jnp.where(pos0 + tile == seq, zero, hn_ref[...])


def _dwconv(u, w, b, tile, left):
    taps = w.shape[0]
    n = u.shape[0]
    shifted = {0: u}
    for d in range(1, taps - left):
        shifted[d] = pltpu.roll(shifted[d - 1], n - 1, axis=0)
    for d in range(1, left + 1):
        shifted[-d] = pltpu.roll(shifted[1 - d], 1, axis=0)
    y = b
    for kk in range(taps):
        y = y + w[kk:kk + 1, :] * shifted[kk - left][HALO:HALO + tile, :]
    return y


def _slab_conv3(u, w, b, tile):
    s = tile // SUB
    slabs = [u[r * s:(r + 1) * s, :] for r in range(SUB)]
    row = lax.broadcasted_iota(jnp.int32, slabs[0].shape, 0)
    before = jnp.where(row == 0, u[tile:tile + 1, :], pltpu.roll(slabs[SUB - 1], 1, axis=0))
    after = jnp.where(row == s - 1, u[tile + 1:tile + 2, :], pltpu.roll(slabs[0], s - 1, axis=0))
    out = []
    for r in range(SUB):
        left = slabs[r - 1] if r > 0 else before
        right = slabs[r + 1] if r < SUB - 1 else after
        out.append(b + w[0:1, :] * left + w[1:2, :] * slabs[r] + w[2:3, :] * right)
    return jnp.concatenate(out, axis=0)


def _ffn_kernel(*refs, tile, seq, emit_next, slab_in):
    if emit_next:
        (hp_ref, hm_ref, hn_ref, x_ref, wg_ref, wv_ref, cwg_ref, cwv_ref, cbg_ref, cbv_ref, wd_ref,
         gpost_ref, gnext_ref, xo_ref, ho_ref, hh_ref, acc_ref, nat_ref) = refs
    else:
        (hp_ref, hm_ref, hn_ref, x_ref, wg_ref, wv_ref, cwg_ref, cwv_ref, cbg_ref, cbv_ref, wd_ref,
         gpost_ref, xo_ref, hh_ref, acc_ref, nat_ref) = refs
    i = pl.program_id(0)
    c = pl.program_id(1)
    s = tile // SUB
    ng = nat_ref.shape[0]

    @pl.when(c == 0)
    def _():
        pos0 = lax.rem(i * tile, seq)
        if slab_in:
            hh_ref[0:tile, :] = hm_ref[...]
        else:
            hh_ref[0:tile, :] = _to_slabs(hm_ref[...], nat_ref).astype(BF16)
        zero = jnp.zeros((1, hm_ref.shape[1]), F32)
        prev = jnp.where(pos0 == 0, zero, hp_ref[HALO - 1:HALO, :].astype(F32))
        nxt = jnp.where(pos0 + tile == seq, zero, hn_ref[0:1, :].astype(F32))
        tail = jnp.concatenate([prev, nxt, jnp.zeros((HALO - 2, hm_ref.shape[1]), F32)], axis=0)
        hh_ref[tile:, :] = tail.astype(BF16)
        acc_ref[...] = jnp.zeros(acc_ref.shape, F32)

    hh = hh_ref[...]
    g = _slab_conv3(_dot(hh, wg_ref[...]), cwg_ref[...], cbg_ref[...], tile)
    val = _slab_conv3(_dot(hh, wv_ref[...]), cwv_ref[...], cbv_ref[...], tile)
    a = (_gelu_tanh(g) * val).astype(BF16)
    acc_ref[...] += _dot(a, wd_ref[...])

    @pl.when(c == pl.num_programs(1) - 1)
    def _():
        f = _rms(acc_ref[...], gpost_ref[...])
        for r in range(SUB):
            for k in range(ng):
                nat_ref[k, pl.ds(r, s, stride=SUB), :] = f[r * s:(r + 1) * s, k * LANES:(k + 1) * LANES]
        xo = x_ref[...] + jnp.concatenate([nat_ref[k] for k in range(ng)], axis=1)
        xo_ref[...] = xo
        if emit_next:
            ho_ref[...] = _rms(xo, gnext_ref[...]).astype(BF16)


def _ffn(h, x, w_up, conv_w, conv_b, w_down, g_post, g_next, *, layer, tile, fc, seq, slab_in):
    n, d = x.shape
    d_ff = w_down.shape[1]
    nc = d_ff // fc
    kk = conv_w.shape[1]
    assert kk == 3 and d % LANES == 0 and tile % (SUB * SUB) == 0
    emit_next = g_next is not None
    tok = pl.BlockSpec((tile, d), lambda i, c: (i, 0))
    in_specs = _halo_specs(tile, d, n) + [
        tok,
        pl.BlockSpec((None, d, fc), lambda i, c: (layer, 0, c)),
        pl.BlockSpec((None, d, fc), lambda i, c: (layer, 0, nc + c)),
        pl.BlockSpec((None, kk, fc), lambda i, c: (layer, 0, c)),
        pl.BlockSpec((None, kk, fc), lambda i, c: (layer, 0, nc + c)),
        pl.BlockSpec((1, fc), lambda i, c: (0, c)),
        pl.BlockSpec((1, fc), lambda i, c: (0, nc + c)),
        pl.BlockSpec((None, fc, d), lambda i, c: (layer, c, 0)),
        _const_spec(g_post.shape)]
    args = [h, h, h, x, w_up, w_up, conv_w, conv_w, conv_b, conv_b, w_down, g_post]
    out_specs = [tok]
    out_shape = [jax.ShapeDtypeStruct((n, d), F32)]
    if emit_next:
        in_specs.append(_const_spec(g_next.shape))
        args.append(g_next)
        out_specs.append(tok)
        out_shape.append(jax.ShapeDtypeStruct((n, d), BF16))
    res = pl.pallas_call(
        functools.partial(_ffn_kernel, tile=tile, seq=seq, emit_next=emit_next, slab_in=slab_in),
        grid=(n // tile, nc),
        in_specs=in_specs,
        out_specs=out_specs,
        out_shape=out_shape,
        scratch_shapes=[pltpu.VMEM((tile + HALO, d), BF16), pltpu.VMEM((tile, d), F32),
                        pltpu.VMEM((d // LANES, tile, LANES), F32)],
        compiler_params=_params(("parallel", "arbitrary")),
        name="conv_ffn",
    )(*args)
    return (res[0], res[1]) if emit_next else (res[0], None)


def _lru_coeffs(xc, xc_bf, gw_ref, gab_ref, gxb_ref, lam_ref):
    nb, bw, _ = gw_ref.shape
    neg = -lam_ref[...]
    sp = jnp.maximum(neg, 0.0) + jnp.log1p(jnp.exp(-jnp.abs(neg)))
    half_c = (-0.5 * LRU_C) * sp
    half_gab = 0.5 * gab_ref[...]
    half_gxb = 0.5 * gxb_ref[...]
    a_parts, b_parts = [], []
    for blk in range(nb):
        sl = slice(blk * bw, (blk + 1) * bw)
        rg = _dot(xc_bf[:, sl], gw_ref[blk])
        hc = half_c[:, sl]
        log_a = hc * jnp.tanh(rg[:, :bw] + half_gab[:, sl]) + hc
        hx = 0.5 * xc[:, sl]
        gated_x = hx * jnp.tanh(rg[:, bw:] + half_gxb[:, sl]) + hx
        t = jnp.tanh(log_a)
        one_minus_a2 = (-2.0 * t) / (1.0 - t)
        root = jnp.where(one_minus_a2 > 0.0, one_minus_a2 * lax.rsqrt(one_minus_a2), 0.0)
        a_parts.append(jnp.exp2(log_a * LOG2E))
        b_parts.append(root * gated_x)
    return jnp.concatenate(a_parts, axis=1), jnp.concatenate(b_parts, axis=1)


def _scan_tile(a_ref, b_ref, carry_ref, *, reverse):
    t, ch = a_ref.shape
    n_chunks = t // SUB
    row = lax.broadcasted_iota(jnp.int32, (SUB, ch), 0)

    def body(j, carry):
        jj = (n_chunks - 1 - j) if reverse else j
        r0 = pl.multiple_of(jj * SUB, SUB)
        a = a_ref[pl.ds(r0, SUB), :]
        b = b_ref[pl.ds(r0, SUB), :]
        for dist in (1, 2, 4):
            shift = (SUB - dist) if reverse else dist
            a_sh = pltpu.roll(a, shift, axis=0)
            b_sh = pltpu.roll(b, shift, axis=0)
            valid = (row < SUB - dist) if reverse else (row >= dist)
            b = jnp.where(valid, a * b_sh + b, b)
            a = jnp.where(valid, a * a_sh, a)
        hcur = a * carry + b
        b_ref[pl.ds(r0, SUB), :] = hcur
        edge = hcur[0:1, :] if reverse else hcur[SUB - 1:SUB, :]
        return jnp.broadcast_to(edge, (SUB, ch))

    carry_ref[...] = lax.fori_loop(0, n_chunks, body, carry_ref[...], unroll=4)


def _rec_fwd_kernel(hp_ref, hm_ref, hn_ref, wg_ref, wx_ref, cw_ref, cb_ref, gw_ref, gab_ref, gxb_ref,
                    lam_ref, gate_ref, xc_ref, hf_ref, hh_ref, a_ref, b_ref, carry_ref, *, tile, seq):
    pos0 = lax.rem(pl.program_id(0) * tile, seq)
    _fill_halo(hh_ref, hp_ref, hm_ref, hn_ref, pos0, tile, seq)

    @pl.when(pos0 == 0)
    def _():
        carry_ref[...] = jnp.zeros(carry_ref.shape, F32)

    gate_ref[...] = _dot(hm_ref[...], wg_ref[...]).astype(BF16)
    left = (cw_ref.shape[0] - 1) // 2
    xc = _dwconv(_dot(hh_ref[...], wx_ref[...]), cw_ref[...], cb_ref[...], tile, left)
    xc_bf = xc.astype(BF16)
    xc_ref[...] = xc_bf
    a, b = _lru_coeffs(xc, xc_bf, gw_ref, gab_ref, gxb_ref, lam_ref)
    a_ref[...] = a
    b_ref[...] = b
    _scan_tile(a_ref, b_ref, carry_ref, reverse=False)
    hf_ref[...] = b_ref[...].astype(BF16)


def _rec_fwd(h, w_g, w_x, conv_w, conv_b, gw, gab, gxb, lam, *, tile, seq):
    n, d = h.shape
    d_rnn = w_x.shape[1]
    tok = pl.BlockSpec((tile, d_rnn), lambda i: (i, 0))
    in_specs = _halo_specs(tile, d, n) + [_const_spec(a.shape) for a in
                                          (w_g, w_x, conv_w, conv_b, gw, gab, gxb, lam)]
    out = jax.ShapeDtypeStruct((n, d_rnn), BF16)
    return pl.pallas_call(
        functools.partial(_rec_fwd_kernel, tile=tile, seq=seq),
        grid=(n // tile,),
        in_specs=in_specs,
        out_specs=[tok, tok, tok],
        out_shape=[out, out, out],
        scratch_shapes=[pltpu.VMEM((tile + 2 * HALO, d), BF16), pltpu.VMEM((tile, d_rnn), F32),
                        pltpu.VMEM((tile, d_rnn), F32), pltpu.VMEM((SUB, d_rnn), F32)],
        compiler_params=_params(("arbitrary",)),
        name="rglru_fwd",
    )(h, h, h, w_g, w_x, conv_w, conv_b, gw, gab, gxb, lam)


def _rec_bwd_kernel(xc_ref, gate_ref, hf_ref, x_ref, gw_ref, gab_ref, gxb_ref, lam_ref, wo_ref,
                    gpost_ref, gnext_ref, xo_ref, ho_ref, a_ref, b_ref, carry_ref, *, tile, seq):
    i = pl.num_programs(0) - 1 - pl.program_id(0)
    pos0 = lax.rem(i * tile, seq)

    @pl.when(pos0 + tile == seq)
    def _():
        carry_ref[...] = jnp.zeros(carry_ref.shape, F32)

    xc_bf = xc_ref[...]
    a, b = _lru_coeffs(xc_bf.astype(F32), xc_bf, gw_ref, gab_ref, gxb_ref, lam_ref)
    a_ref[...] = a
    b_ref[...] = b
    _scan_tile(a_ref, b_ref, carry_ref, reverse=True)
    y = (hf_ref[...].astype(F32) + b_ref[...]) * _gelu_tanh(gate_ref[...].astype(F32))
    m = _dot(y.astype(BF16), wo_ref[...])
    xo = x_ref[...] + _rms(m, gpost_ref[...])
    xo_ref[...] = xo
    ho_ref[...] = _rms(xo, gnext_ref[...]).astype(BF16)


def _rec_bwd(xc, gate, hf, x, gw, gab, gxb, lam, w_out, g_post, g_next, *, tile, seq):
    n, d = x.shape
    d_rnn = xc.shape[1]
    nt = n // tile
    rtok = lambda w: pl.BlockSpec((tile, w), lambda i: (nt - 1 - i, 0))
    in_specs = [rtok(d_rnn), rtok(d_rnn), rtok(d_rnn), rtok(d)] + [
        _const_spec(a.shape) for a in (gw, gab, gxb, lam, w_out, g_post, g_next)]
    return pl.pallas_call(
        functools.partial(_rec_bwd_kernel, tile=tile, seq=seq),
        grid=(nt,),
        in_specs=in_specs,
        out_specs=[rtok(d), rtok(d)],
        out_shape=[jax.ShapeDtypeStruct((n, d), F32), jax.ShapeDtypeStruct((n, d), BF16)],
        scratch_shapes=[pltpu.VMEM((tile, d_rnn), F32), pltpu.VMEM((tile, d_rnn), F32),
                        pltpu.VMEM((SUB, d_rnn), F32)],
        compiler_params=_params(("arbitrary",)),
        name="rglru_bwd",
    )(xc, gate, hf, x, gw, gab, gxb, lam, w_out, g_post, g_next)


def _row(v):
    return v.reshape(1, -1).astype(F32)


def _rope_tables(s_max):
    pos = jnp.arange(s_max, dtype=F32)
    inv = ROPE_BASE ** (-jnp.arange(0, MLA_ROPE, 2, dtype=F32) / MLA_ROPE)
    ang = pos[:, None] * inv[None, :]
    cos, sin = jnp.cos(ang), jnp.sin(ang)
    pad = LANES - MLA_NOPE - MLA_ROPE
    cos_t = jnp.concatenate([jnp.ones((s_max, MLA_NOPE), F32), cos, cos, jnp.zeros((s_max, pad), F32)], axis=1)
    sin_t = jnp.concatenate([jnp.zeros((s_max, MLA_NOPE), F32), -sin, sin, jnp.zeros((s_max, pad), F32)], axis=1)
    return cos_t, sin_t


def _attn_weights(w_in, w_uq, w_ukv):
    d = w_in.shape[0]
    q_lora = w_uq.shape[0]
    kv_lora = w_ukv.shape[0]
    o1 = q_lora + kv_lora
    o2 = o1 + MLA_ROPE
    pad = LANES - MLA_NOPE - MLA_ROPE
    kpe = w_in[:, o1:o2]
    zl = jnp.zeros((d, MLA_NOPE), w_in.dtype)
    zr = jnp.zeros((d, pad), w_in.dtype)
    w_all = jnp.concatenate([w_in[:, :o1], zl, kpe, zr, w_in[:, o2:]], axis=1)

    wq = w_uq.reshape(q_lora, MLA_HEADS, MLA_NOPE + MLA_ROPE)
    zq = jnp.zeros((q_lora, MLA_HEADS, pad), w_uq.dtype)
    wq_p = jnp.concatenate([wq, zq], axis=-1).reshape(q_lora, MLA_HEADS * LANES)

    wkv = w_ukv.reshape(kv_lora, MLA_HEADS, MLA_NOPE + MLA_V)
    zk = jnp.zeros((kv_lora, MLA_HEADS, LANES - MLA_NOPE), w_ukv.dtype)
    wk_p = jnp.concatenate([wkv[..., :MLA_NOPE], zk], axis=-1).reshape(kv_lora, MLA_HEADS * LANES)
    wv = wkv[..., MLA_NOPE:].reshape(kv_lora, MLA_HEADS * MLA_V)
    return tuple(a.astype(BF16) for a in (w_all, wq_p, wk_p, wv))


TOKEN_TILE = 1024
FFN_TILE = 1024
FFN_CHUNK = 1024
REC_TILE = 1024
MLA_SCORE_ELEMS = 2 ** 21


def kernel(x_prompt, x_sample, norm_mix_pre, norm_mix_post, norm_ffn_pre, norm_ffn_post, w_in_attn, q_norm, w_uq, kv_norm, w_ukv, na_rpb, w_out_attn, w_in_rec, conv_w_rec, conv_b_rec, gate_a_w, gate_a_b, gate_x_w, gate_x_b, lru_lambda, w_out_rec, w_ffn_up, conv_w_ffn, conv_b_ffn, w_ffn_down):
    depth = norm_mix_pre.shape[0]
    d_rnn = w_out_rec.shape[1]
    n_mla = MLA_HEADS * MLA_V
    cos_t, sin_t = _rope_tables(max(x_prompt.shape[1], x_sample.shape[1]))

    ffn_up = w_ffn_up.astype(BF16)
    ffn_down = w_ffn_down.astype(BF16)
    layers = []
    for li in range(depth):
        j = li // 2
        lw = dict(ffn_conv_b=_row(conv_b_ffn[li]),
                  g_pre=_row(norm_mix_pre[li]), g_post=_row(norm_mix_post[li]),
                  g_ffn_pre=_row(norm_ffn_pre[li]), g_ffn_post=_row(norm_ffn_post[li]),
                  g_next=_row(norm_mix_pre[li + 1]) if li + 1 < depth else None)
        if li % 2 == 0:
            lw["attn"] = _attn_weights(w_in_attn[j], w_uq[j], w_ukv[j])
            lw["q_norm"] = _row(q_norm[j])
            lw["kv_norm"] = _row(kv_norm[j])
            lw["bias"] = _na_bias_table(na_rpb[j])
            w_o = w_out_attn[j].astype(BF16)
            lw["out"] = (w_o[:n_mla], w_o[n_mla:])
        else:
            w_in = w_in_rec[j].astype(BF16)
            lw["rec_in"] = (w_in[:, :d_rnn], w_in[:, d_rnn:])
            lw["rec_conv"] = (conv_w_rec[j], _row(conv_b_rec[j]))
            lw["dirs"] = [((0.5 * jnp.concatenate([gate_a_w[j, e], gate_x_w[j, e]], axis=-1)).astype(BF16),
                           _row(gate_a_b[j, e]), _row(gate_x_b[j, e]), _row(lru_lambda[j, e])) for e in range(2)]
            lw["rec_out"] = w_out_rec[j].astype(BF16)
        layers.append(lw)

    def trunk(x3):
        batch, seq, d = x3.shape
        x = x3.reshape(batch * seq, d)
        tile = min(TOKEN_TILE, seq)
        ffn_tile = min(FFN_TILE, seq)
        h = None
        for li, lw in enumerate(layers):
            if li % 2 == 0:
                w_all, wq_p, wk_p, wv = lw["attn"]
                q, k, v, naq, nak, nav = _attn_in(x, lw["g_pre"], w_all, lw["q_norm"], wq_p, lw["kv_norm"], wk_p, wv,
                                                  cos_t, sin_t, tile=tile, seq=seq)
                o_mla = _mla(q, k, v, batch=batch, seq=seq, tq=min(seq, MLA_SCORE_ELEMS // seq))
                o_na = _na(naq, nak, nav, lw["bias"], batch=batch, seq=seq)
                x, h = _attn_out(o_mla, o_na, lw["out"][0], lw["out"][1], x, lw["g_post"], lw["g_ffn_pre"],
                                 tile=ffn_tile)
            else:
                rec_tile = min(REC_TILE, seq)
                gate, xc, hf = _rec_fwd(h, lw["rec_in"][0], lw["rec_in"][1], *lw["rec_conv"], *lw["dirs"][0],
                                        tile=rec_tile, seq=seq)
                x, h = _rec_bwd(xc, gate, hf, x, *lw["dirs"][1], lw["rec_out"], lw["g_post"], lw["g_ffn_pre"],
                                tile=rec_tile, seq=seq)
            x, h = _ffn(h, x, ffn_up, conv_w_ffn, lw["ffn_conv_b"], ffn_down, lw["g_ffn_post"], lw["g_next"],
                        layer=li, tile=ffn_tile, fc=min(FFN_CHUNK, ffn_down.shape[1]), seq=seq, slab_in=li % 2 == 0)
        return x.reshape(batch, seq, d)

    return trunk(x_prompt), trunk(x_sample)
```
